```python
import math
import jax, jax.numpy as jnp
from jax import lax
import numpy as np

D_MODEL = 2048
BATCH = 16
SEQ = 2048
DEPTH = 4

GRID_W = 64
CTX_LEN = 256
HEAD_DIM = 128
EPS = 1e-6
A_Q_HEADS = 12
A_KV_HEADS = 4
A_GROUP = A_Q_HEADS // A_KV_HEADS
WINDOW = 128
BLOCK = 128
ROPE_BASE = 10000.0
B_GROUPS = 4
B_GROUP_DIM = 128
C_HEADS = 16
C_DK = 128
C_DV = 128
C_CONV = 5
C_CHUNK = 64
N_EXPERTS = 16
N_GROUPS = 4
EXPERTS_PER_GROUP = N_EXPERTS // N_GROUPS
TOP_K = 2
D_FF_EXPERT = 1024

A_Q_W = A_Q_HEADS * HEAD_DIM
A_KV_W = A_KV_HEADS * HEAD_DIM
B_W = B_GROUPS * B_GROUP_DIM
EVEN_IN = A_Q_W + 2 * A_KV_W + B_W
EVEN_OUT = A_Q_W + B_W
C_QK_W = C_HEADS * C_DK
C_V_W = C_HEADS * C_DV
ODD_CONV = 2 * C_QK_W + C_V_W
ODD_IN = ODD_CONV + C_V_W + 4 * C_HEADS
N_EVEN = (DEPTH + 1) // 2
N_ODD = DEPTH // 2

kernel_name = "hybrid_swa_fourier_deltanet_groupmoe_prefix_dit"

F32 = jnp.float32


def _rmsnorm(x, w):
    xf = x.astype(F32)
    y = xf * lax.rsqrt(jnp.mean(xf * xf, axis=-1, keepdims=True) + EPS)
    return (y * w.astype(F32)).astype(x.dtype)


def _l2norm(x):
    xf = x.astype(F32)
    return xf * lax.rsqrt(jnp.sum(xf * xf, axis=-1, keepdims=True) + EPS)


def _axial_rope_angles(n_tok):
    rows = n_tok // GRID_W
    r, col = jnp.meshgrid(jnp.arange(rows), jnp.arange(GRID_W), indexing="ij")
    r = r.reshape(-1).astype(F32)
    col = col.reshape(-1).astype(F32)
    n_freq = HEAD_DIM // 4
    inv = ROPE_BASE ** (-jnp.arange(n_freq, dtype=F32) / n_freq)
    return r[:, None] * inv, col[:, None] * inv


def _rotate(xs, ang):
    nf = ang.shape[-1]
    x1, x2 = xs[..., :nf], xs[..., nf:]
    cos = jnp.cos(ang)[None, :, None, :]
    sin = jnp.sin(ang)[None, :, None, :]
    return jnp.concatenate([x1 * cos - x2 * sin, x2 * cos + x1 * sin], axis=-1)


def _apply_axial_rope(x, ang_r, ang_c):
    xf = x.astype(F32)
    half = HEAD_DIM // 2
    out = jnp.concatenate([_rotate(xf[..., :half], ang_r), _rotate(xf[..., half:], ang_c)], axis=-1)
    return out.astype(x.dtype)


def _attend_with_sink(qg, k, v, sink_g, mask):
    scale = qg.shape[-1] ** -0.5
    s = jnp.einsum('bqhgd,bkhd->bhgqk', qg.astype(F32), k.astype(F32)) * scale
    if mask is not None:
        s = jnp.where(mask, s, -jnp.inf)
    sk = sink_g.astype(F32)[None, :, :, None]
    m = jnp.maximum(s.max(-1), sk)
    p = jnp.exp(s - m[..., None])
    den = p.sum(-1) + jnp.exp(sk - m)
    o = jnp.einsum('bhgqk,bkhd->bhgqd', p, v.astype(F32)) / den[..., None]
    return o.transpose(0, 3, 1, 2, 4).astype(qg.dtype)


def _window_attention(q, k, v, kc, vc, sink_g):
    Bsz, N, _, hd = q.shape
    nb = N // BLOCK
    pad = ((0, 0), (BLOCK, BLOCK), (0, 0), (0, 0))
    kp = jnp.pad(k, pad).reshape(Bsz, nb + 2, BLOCK, A_KV_HEADS, hd)
    vp = jnp.pad(v, pad).reshape(Bsz, nb + 2, BLOCK, A_KV_HEADS, hd)
    kwin = jnp.concatenate([kp[:, :-2], kp[:, 1:-1], kp[:, 2:]], axis=2)
    vwin = jnp.concatenate([vp[:, :-2], vp[:, 1:-1], vp[:, 2:]], axis=2)
    qb = q.reshape(Bsz, nb, BLOCK, A_KV_HEADS, A_GROUP, hd)
    kpos = jnp.arange(3 * BLOCK) - BLOCK
    rel = kpos[None, :] - jnp.arange(BLOCK)[:, None]
    band = jnp.abs(rel) <= WINDOW
    ctx_mask = jnp.ones((BLOCK, kc.shape[1]), dtype=bool)

    def one_block(args):
        qi, ki, vi, bi = args
        kabs = bi * BLOCK + kpos
        valid = band & ((kabs >= 0) & (kabs < N))[None, :]
        kk = jnp.concatenate([ki, kc], axis=1)
        vv = jnp.concatenate([vi, vc], axis=1)
        mask = jnp.concatenate([valid, ctx_mask], axis=1)
        return _attend_with_sink(qi, kk, vv, sink_g, mask)

    out = lax.map(one_block, (qb.swapaxes(0, 1), kwin.swapaxes(0, 1), vwin.swapaxes(0, 1), jnp.arange(nb)))
    return out.swapaxes(0, 1).reshape(Bsz, N, A_Q_W)


def _fourier_mix(f):
    Bsz, T, _ = f.shape
    fg = f.astype(F32).reshape(Bsz, T, B_GROUPS, B_GROUP_DIM)
    y = jnp.fft.fft2(fg, axes=(1, 3), norm="ortho").real
    return y.reshape(Bsz, T, B_W).astype(f.dtype)


def _even_mixer(u_lat, u_ctx, w_in, w_out, sink, ang_r, ang_c, need_ctx):
    Bsz, N, _ = u_lat.shape
    L = u_ctx.shape[1]
    sink_g = sink.reshape(A_KV_HEADS, A_GROUP)
    p = u_lat @ w_in
    q = p[..., :A_Q_W].reshape(Bsz, N, A_Q_HEADS, HEAD_DIM)
    k = p[..., A_Q_W:A_Q_W + A_KV_W].reshape(Bsz, N, A_KV_HEADS, HEAD_DIM)
    v = p[..., A_Q_W + A_KV_W:A_Q_W + 2 * A_KV_W].reshape(Bsz, N, A_KV_HEADS, HEAD_DIM)
    f = p[..., A_Q_W + 2 * A_KV_W:]
    q = _apply_axial_rope(q, ang_r, ang_c)
    k = _apply_axial_rope(k, ang_r, ang_c)
    if need_ctx:
        pc = u_ctx @ w_in
    else:
        pc = u_ctx @ w_in[:, :A_Q_W + 2 * A_KV_W]
    kc = pc[..., A_Q_W:A_Q_W + A_KV_W].reshape(Bsz, L, A_KV_HEADS, HEAD_DIM)
    vc = pc[..., A_Q_W + A_KV_W:A_Q_W + 2 * A_KV_W].reshape(Bsz, L, A_KV_HEADS, HEAD_DIM)
    a_lat = _window_attention(q, k, v, kc, vc, sink_g)
    y_lat = jnp.concatenate([a_lat, _fourier_mix(f)], axis=-1) @ w_out
    y_ctx = None
    if need_ctx:
        qc = pc[..., :A_Q_W].reshape(Bsz, L, A_KV_HEADS, A_GROUP, HEAD_DIM)
        a_ctx = _attend_with_sink(qc, kc, vc, sink_g, None).reshape(Bsz, L, A_Q_W)
        y_ctx = jnp.concatenate([a_ctx, _fourier_mix(pc[..., A_Q_W + 2 * A_KV_W:])], axis=-1) @ w_out
    return y_lat, y_ctx


def _short_conv(u, w):
    padw = w.shape[0] // 2
    out = lax.conv_general_dilated(u, w[:, None, :].astype(u.dtype), window_strides=(1,),
                                   padding=[(padw, padw)], dimension_numbers=('NWC', 'WIO', 'NWC'),
                                   feature_group_count=u.shape[-1])
    return jax.nn.silu(out)


def _to_chunks(x):
    Bsz, T, H = x.shape[:3]
    y = x.reshape((Bsz, T // C_CHUNK, C_CHUNK, H) + x.shape[3:])
    return y.transpose((1, 0, 3, 2) + tuple(range(4, y.ndim)))


def _gated_delta_chunked(q, k, v, g, beta, s0):
    Bsz, T, H, dk = q.shape
    dv = v.shape[-1]
    qc = _to_chunks(q * dk ** -0.5)
    kc = _to_chunks(k)
    vc = _to_chunks(v)
    gcum = jnp.cumsum(_to_chunks(g), axis=-1)
    bc = _to_chunks(beta)
    kb = kc * bc[..., None]
    vb = vc * bc[..., None]
    incl = jnp.tril(jnp.ones((C_CHUNK, C_CHUNK), dtype=bool))
    strict = jnp.tril(jnp.ones((C_CHUNK, C_CHUNK), dtype=bool), -1)
    decay = jnp.exp(jnp.where(incl, gcum[..., :, None] - gcum[..., None, :], -jnp.inf))
    nmat = jnp.where(strict, jnp.einsum('nbhid,nbhjd->nbhij', kb, kc) * decay, 0.0)
    eye = jnp.eye(C_CHUNK, dtype=F32)
    tmat = lax.linalg.triangular_solve(eye + nmat, jnp.broadcast_to(eye, nmat.shape),
                                       left_side=True, lower=True, unit_diagonal=True)
    u = tmat @ vb
    w = tmat @ (kb * jnp.exp(gcum)[..., None])
    a_qk = jnp.where(incl, jnp.einsum('nbhid,nbhjd->nbhij', qc, kc) * decay, 0.0)

    def step(S, xs):
        qi, ki, ui, wi, gi, ai = xs
        v_new = ui - wi @ S
        o = (qi * jnp.exp(gi)[..., None]) @ S + ai @ v_new
        g_last = gi[..., -1]
        k_dec = ki * jnp.exp(g_last[..., None] - gi)[..., None]
        S = S * jnp.exp(g_last)[..., None, None] + jnp.einsum('bhcd,bhce->bhde', k_dec, v_new)
        return S, o

    s_fin, o = lax.scan(step, s0, (qc, kc, u, w, gcum, a_qk))
    o = o.transpose(1, 0, 3, 2, 4).reshape(Bsz, T, H, dv)
    return o, s_fin


def _delta_direction(ctx_in, lat_in, reverse):
    if reverse:
        ctx_in = tuple(jnp.flip(t, axis=1) for t in ctx_in)
        lat_in = tuple(jnp.flip(t, axis=1) for t in lat_in)
    Bsz = ctx_in[0].shape[0]
    s0 = jnp.zeros((Bsz, C_HEADS, C_DK, C_DV), F32)
    o_ctx, s_ctx = _gated_delta_chunked(*ctx_in, s0)
    o_lat, _ = _gated_delta_chunked(*lat_in, s_ctx)
    if reverse:
        o_ctx = jnp.flip(o_ctx, axis=1)
        o_lat = jnp.flip(o_lat, axis=1)
    return o_ctx, o_lat


def _delta_inputs(h, w_in, conv_w, a_log, dt_bias):
    Bsz, T, _ = h.shape
    p = h @ w_in
    qkv = _short_conv(p[..., :ODD_CONV], conv_w)
    z = p[..., ODD_CONV:ODD_CONV + C_V_W]
    gates = p[..., ODD_CONV + C_V_W:].astype(F32).reshape(Bsz, T, 4, C_HEADS)
    g = -jnp.exp(a_log.astype(F32)) * jax.nn.softplus(gates[:, :, 0:2] + dt_bias.astype(F32))
    beta = jax.nn.sigmoid(gates[:, :, 2:4])
    q = _l2norm(qkv[..., :C_QK_W].reshape(Bsz, T, C_HEADS, C_DK))
    k = _l2norm(qkv[..., C_QK_W:2 * C_QK_W].reshape(Bsz, T, C_HEADS, C_DK))
    v = qkv[..., 2 * C_QK_W:].reshape(Bsz, T, C_HEADS, C_DV).astype(F32)
    return q, k, v, z, g, beta


def _gated_out(o, z, norm_w, w_out, dtype):
    Bsz, T = o.shape[:2]
    on = o * lax.rsqrt(jnp.mean(o * o, axis=-1, keepdims=True) + EPS) * norm_w.astype(F32)
    zg = jax.nn.silu(z.astype(F32)).reshape(Bsz, T, C_HEADS, C_DV)
    return (on * zg).reshape(Bsz, T, C_V_W).astype(dtype) @ w_out


def _odd_mixer(u_lat, u_ctx, w_in, conv_w, a_log, dt_bias, norm_w, w_out, need_ctx):
    ql, kl, vl, zl, gl, bl = _delta_inputs(u_lat, w_in, conv_w, a_log, dt_bias)
    qc, kc, vc, zc, gc, bc = _delta_inputs(u_ctx, w_in, conv_w, a_log, dt_bias)
    oc_f, ol_f = _delta_direction((qc, kc, vc, gc[:, :, 0], bc[:, :, 0]), (ql, kl, vl, gl[:, :, 0], bl[:, :, 0]), False)
    oc_b, ol_b = _delta_direction((qc, kc, vc, gc[:, :, 1], bc[:, :, 1]), (ql, kl, vl, gl[:, :, 1], bl[:, :, 1]), True)
    y_lat = _gated_out(ol_f + ol_b, zl, norm_w, w_out, u_lat.dtype)
    y_ctx = _gated_out(oc_f + oc_b, zc, norm_w, w_out, u_ctx.dtype) if need_ctx else None
    return y_lat, y_ctx


def _route(h, router_w, router_b):
    scores = jax.nn.sigmoid((h @ router_w).astype(F32))
    sel = scores + router_b.astype(F32)
    grp = sel.reshape(-1, N_GROUPS, EXPERTS_PER_GROUP)
    grp_score = lax.top_k(grp, 2)[0].sum(-1)
    best = jnp.argmax(grp_score, axis=-1)
    gmask = jnp.repeat(jax.nn.one_hot(best, N_GROUPS, dtype=F32), EXPERTS_PER_GROUP, axis=-1) > 0
    _, idx = lax.top_k(jnp.where(gmask, sel, -jnp.inf), TOP_K)
    w_sel = jnp.take_along_axis(scores, idx, axis=-1)
    w_sel = w_sel / jnp.sum(w_sel, axis=-1, keepdims=True)
    return jnp.sum(jax.nn.one_hot(idx, N_EXPERTS, dtype=F32) * w_sel[..., None], axis=-2)


def _moe(h, router_w, router_b, wg, wu, wd):
    comb = _route(h, router_w, router_b).astype(h.dtype)
    y = jnp.zeros_like(h)
    for e in range(N_EXPERTS):
        a = jax.nn.silu(h @ wg[e]) * (h @ wu[e])
        y = y + comb[:, e:e + 1] * (a @ wd[e])
    return y


def setup_inputs(seed: int = 0) -> dict:
    key = jax.random.key(seed)
    ks = jax.random.split(key, 24)

    def nrm(k, shape, scale):
        return jax.random.normal(k, shape, F32) * scale

    dt = jnp.exp(jax.random.uniform(ks[14], (N_ODD, 2, C_HEADS), F32, math.log(1e-3), math.log(1e-1)))
    return {
        "x": nrm(ks[0], (BATCH, SEQ, D_MODEL), 1.0),
        "c": nrm(ks[1], (BATCH, D_MODEL), 1.0),
        "ctx": nrm(ks[2], (BATCH, CTX_LEN, D_MODEL), 1.0),
        "c_ctx": nrm(ks[3], (D_MODEL,), 1.0),
        "adaln_w": nrm(ks[4], (DEPTH, D_MODEL, 6 * D_MODEL), 0.5 * D_MODEL ** -0.5),
        "adaln_b": nrm(ks[5], (DEPTH, 6 * D_MODEL), 0.02),
        "norm_mix_w": 1.0 + nrm(ks[6], (DEPTH, D_MODEL), 0.02),
        "norm_ffn_w": 1.0 + nrm(ks[7], (DEPTH, D_MODEL), 0.02),
        "attn_in_w": nrm(ks[8], (N_EVEN, D_MODEL, EVEN_IN), D_MODEL ** -0.5),
        "attn_out_w": nrm(ks[9], (N_EVEN, EVEN_OUT, D_MODEL), EVEN_OUT ** -0.5),
        "attn_sink": nrm(ks[10], (N_EVEN, A_Q_HEADS), 0.5),
        "dn_in_w": nrm(ks[11], (N_ODD, D_MODEL, ODD_IN), D_MODEL ** -0.5),
        "dn_conv_w": nrm(ks[12], (N_ODD, C_CONV, ODD_CONV), C_CONV ** -0.5),
        "dn_a_log": jnp.log(jax.random.uniform(ks[13], (N_ODD, 2, C_HEADS), F32, 1.0, 16.0)),
        "dn_dt_bias": dt + jnp.log(-jnp.expm1(-dt)),
        "dn_norm_w": 1.0 + nrm(ks[15], (N_ODD, C_DV), 0.02),
        "dn_out_w": nrm(ks[16], (N_ODD, C_V_W, D_MODEL), C_V_W ** -0.5),
        "router_w": nrm(ks[17], (D_MODEL, N_EXPERTS), D_MODEL ** -0.5),
        "router_b": nrm(ks[18], (N_EXPERTS,), 0.01),
        "exp_gate_w": nrm(ks[19], (DEPTH, N_EXPERTS, D_MODEL, D_FF_EXPERT), D_MODEL ** -0.5),
        "exp_up_w": nrm(ks[20], (DEPTH, N_EXPERTS, D_MODEL, D_FF_EXPERT), D_MODEL ** -0.5),
        "exp_down_w": nrm(ks[21], (DEPTH, N_EXPERTS, D_FF_EXPERT, D_MODEL), D_FF_EXPERT ** -0.5),
        "final_norm_w": 1.0 + nrm(ks[22], (D_MODEL,), 0.02),
    }


def reference(x, c, ctx, c_ctx, adaln_w, adaln_b, norm_mix_w, norm_ffn_w, attn_in_w, attn_out_w, attn_sink,
              dn_in_w, dn_conv_w, dn_a_log, dn_dt_bias, dn_norm_w, dn_out_w, router_w, router_b,
              exp_gate_w, exp_up_w, exp_down_w, final_norm_w):
    Bsz, N, D = x.shape
    L = ctx.shape[1]
    ang_r, ang_c = _axial_rope_angles(N)
    sc = jax.nn.silu(c)
    scc = jax.nn.silu(c_ctx)
    x_lat, x_ctx = x, ctx
    for layer in range(DEPTH):
        need_ctx = layer < DEPTH - 1
        mod_lat = (sc @ adaln_w[layer] + adaln_b[layer])[:, None, :]
        mod_ctx = (scc @ adaln_w[layer] + adaln_b[layer])[None, None, :]
        sh1, s1, g1, sh2, s2, g2 = jnp.split(mod_lat, 6, axis=-1)
        csh1, cs1, cg1, csh2, cs2, cg2 = jnp.split(mod_ctx, 6, axis=-1)
        u_lat = _rmsnorm(x_lat, norm_mix_w[layer]) * (1.0 + s1) + sh1
        u_ctx = _rmsnorm(x_ctx, norm_mix_w[layer]) * (1.0 + cs1) + csh1
        if layer % 2 == 0:
            i = layer // 2
            y_lat, y_ctx = _even_mixer(u_lat, u_ctx, attn_in_w[i], attn_out_w[i], attn_sink[i], ang_r, ang_c, need_ctx)
        else:
            i = layer // 2
            y_lat, y_ctx = _odd_mixer(u_lat, u_ctx, dn_in_w[i], dn_conv_w[i], dn_a_log[i], dn_dt_bias[i],
                                      dn_norm_w[i], dn_out_w[i], need_ctx)
        x_lat = x_lat + g1 * y_lat
        v_lat = _rmsnorm(x_lat, norm_ffn_w[layer]) * (1.0 + s2) + sh2
        if need_ctx:
            x_ctx = x_ctx + cg1 * y_ctx
            v_ctx = _rmsnorm(x_ctx, norm_ffn_w[layer]) * (1.0 + cs2) + csh2
            tokens = jnp.concatenate([v_lat.reshape(-1, D), v_ctx.reshape(-1, D)], axis=0)
            ff = _moe(tokens, router_w, router_b, exp_gate_w[layer], exp_up_w[layer], exp_down_w[layer])
            x_lat = x_lat + g2 * ff[:Bsz * N].reshape(Bsz, N, D)
            x_ctx = x_ctx + cg2 * ff[Bsz * N:].reshape(Bsz, L, D)
        else:
            ff = _moe(v_lat.reshape(-1, D), router_w, router_b, exp_gate_w[layer], exp_up_w[layer], exp_down_w[layer])
            x_lat = x_lat + g2 * ff.reshape(Bsz, N, D)
    return _rmsnorm(x_lat, final_norm_w)
```

```python
import functools
import math

import jax
import jax.numpy as jnp
from jax import lax
from jax.experimental import pallas as pl
from jax.experimental.pallas import tpu as pltpu

F32 = jnp.float32
BF16 = jnp.bfloat16
EPS = 1e-6

HEAD_DIM = 128
A_Q_HEADS = 12
A_KV_HEADS = 4
A_GROUP = A_Q_HEADS // A_KV_HEADS
BLOCK = 128
GRID_W = 64
ROPE_BASE = 10000.0
B_GROUPS = 4
B_GROUP_DIM = 128
C_HEADS = 16
C_DK = 128
C_CONV = 5
C_CHUNK = 64
N_EXPERTS = 16
N_GROUPS = 4
EXPERTS_PER_GROUP = N_EXPERTS // N_GROUPS

A_Q_W = A_Q_HEADS * HEAD_DIM
A_KV_W = A_KV_HEADS * HEAD_DIM
B_W = B_GROUPS * B_GROUP_DIM
C_W = C_HEADS * C_DK
GATE_W = 128

V7X_VMEM_LIMIT_BYTES = 56 * 1024 * 1024
NEG_BIG = -1e30


def _tile(pref, *dims, mult=8):
    t = min((pref,) + dims)
    t -= t % mult
    while t > mult and any(d % t for d in dims):
        t -= mult
    assert t >= mult and all(d % t == 0 for d in dims), (pref, dims)
    return t


def _params(*sem):
    return pltpu.CompilerParams(dimension_semantics=sem, vmem_limit_bytes=V7X_VMEM_LIMIT_BYTES)


def _bdot(a, b):
    return jnp.dot(a.astype(BF16), b.astype(BF16), preferred_element_type=F32)


def _bdot_nt(a, b):
    return lax.dot_general(a.astype(BF16), b.astype(BF16), (((1,), (1,)), ((), ())), preferred_element_type=F32)


def _bdot_tn(a, b):
    return lax.dot_general(a.astype(BF16), b.astype(BF16), (((0,), (0,)), ((), ())), preferred_element_type=F32)


def _split(a):
    hi = a.astype(BF16)
    lo = (a - hi.astype(F32)).astype(BF16)
    return hi, lo


def _sigmoid(x):
    return 1.0 / (1.0 + jnp.exp(-x))


def _silu(x):
    return x * _sigmoid(x)


def _adaln_kernel(c_ref, w_ref, b_ref, o_ref):
    sc = _silu(c_ref[...])
    sh, sl = _split(sc)
    wh, wl = _split(w_ref[...])
    acc = (jnp.dot(sh, wh, preferred_element_type=F32) + jnp.dot(sh, wl, preferred_element_type=F32)
           + jnp.dot(sl, wh, preferred_element_type=F32))
    o_ref[...] = acc + b_ref[...]


def _adaln(c_all, adaln_w, adaln_b):
    depth, d, n6 = adaln_w.shape
    rows = c_all.shape[0]
    tn = _tile(512, n6, mult=128)
    return pl.pallas_call(
        _adaln_kernel,
        grid=(depth, n6 // tn),
        in_specs=[pl.BlockSpec((rows, d), lambda l, j: (0, 0)),
                  pl.BlockSpec((None, d, tn), lambda l, j: (l, 0, j)),
                  pl.BlockSpec((None, 1, tn), lambda l, j: (l, 0, j))],
        out_specs=pl.BlockSpec((None, rows, tn), lambda l, j: (l, 0, j)),
        out_shape=jax.ShapeDtypeStruct((depth, rows, n6), F32),
        compiler_params=_params("arbitrary", "arbitrary"),
        name="adaln",
    )(c_all, adaln_w, adaln_b.reshape(depth, 1, n6))


def _norm_mod(x, nw, mod, shift_row, scale_row):
    ms = jnp.mean(x * x, axis=-1, keepdims=True)
    y = x * lax.rsqrt(ms + EPS) * nw
    return y * (1.0 + mod[scale_row:scale_row + 1, :]) + mod[shift_row:shift_row + 1, :]


def _rope_slab(a, cos, sin):
    lane = lax.broadcasted_iota(jnp.int32, a.shape, 1)
    first = (lane % 64) < 32
    partner = jnp.where(first, pltpu.roll(a, 96, 1), pltpu.roll(a, 32, 1))
    return a * cos + partner * sin


def _norm_mm_kernel(x_ref, mod_ref, nw_ref, w_ref, cos_ref, sin_ref, o_ref, u_ref, *, rope_tiles, n_lat_tiles, tn):
    i = pl.program_id(0)
    j = pl.program_id(1)

    @pl.when(j == 0)
    def _():
        u_ref[...] = _norm_mod(x_ref[...], nw_ref[...], mod_ref[...], 0, 1).astype(BF16)

    acc = jnp.dot(u_ref[...], w_ref[...], preferred_element_type=F32)
    if rope_tiles == 0:
        o_ref[...] = acc.astype(o_ref.dtype)
    else:
        do_rope = jnp.logical_and(j < rope_tiles, i < n_lat_tiles)

        @pl.when(do_rope)
        def _():
            cos = cos_ref[...]
            sin = sin_ref[...]
            for s in range(tn // HEAD_DIM):
                sl = slice(s * HEAD_DIM, (s + 1) * HEAD_DIM)
                o_ref[:, sl] = _rope_slab(acc[:, sl], cos, sin).astype(o_ref.dtype)

        @pl.when(jnp.logical_not(do_rope))
        def _():
            o_ref[...] = acc.astype(o_ref.dtype)


def _norm_mm(xs, mod3, nw, w, cos_t, sin_t, *, n_lat, seq, rope_cols, out_dtype=BF16, tn_pref=1024):
    tt, d = xs.shape
    nout = w.shape[1]
    nb = mod3.shape[0] - 1
    tm = _tile(512, seq, tt - n_lat, n_lat)
    tn = _tile(tn_pref, nout, *( (rope_cols,) if rope_cols else ()), mult=128)
    per_seq = seq // tm
    kern = functools.partial(_norm_mm_kernel, rope_tiles=rope_cols // tn, n_lat_tiles=n_lat // tm, tn=tn)
    return pl.pallas_call(
        kern,
        grid=(tt // tm, nout // tn),
        in_specs=[pl.BlockSpec((tm, d), lambda i, j: (i, 0)),
                  pl.BlockSpec((None, 6, d), lambda i, j: (jnp.minimum(i // per_seq, nb), 0, 0)),
                  pl.BlockSpec((1, d), lambda i, j: (0, 0)),
                  pl.BlockSpec((d, tn), lambda i, j: (0, j)),
                  pl.BlockSpec((tm, HEAD_DIM), lambda i, j: (i % per_seq, 0)),
                  pl.BlockSpec((tm, HEAD_DIM), lambda i, j: (i % per_seq, 0))],
        out_specs=pl.BlockSpec((tm, tn), lambda i, j: (i, j)),
        out_shape=jax.ShapeDtypeStruct((tt, nout), out_dtype),
        scratch_shapes=[pltpu.VMEM((tm, d), BF16)],
        compiler_params=_params("arbitrary", "arbitrary"),
        name="norm_mm",
    )(xs, mod3, nw, w, cos_t, sin_t)


def _res_mm_kernel(*refs, n_parts, gate_row):
    a_refs = refs[:n_parts]
    w_refs = refs[n_parts:2 * n_parts]
    x_ref, mod_ref, o_ref = refs[2 * n_parts:]
    acc = jnp.dot(a_refs[0][...], w_refs[0][...], preferred_element_type=F32)
    for a_ref, w_ref in zip(a_refs[1:], w_refs[1:]):
        acc = acc + jnp.dot(a_ref[...], w_ref[...], preferred_element_type=F32)
    o_ref[...] = x_ref[...] + mod_ref[gate_row:gate_row + 1, :] * acc


def _res_mm(parts, weights, xs, mod3, *, gate_row, n_lat, seq):
    tt, d = xs.shape
    nb = mod3.shape[0] - 1
    tm = _tile(512, seq, tt - n_lat, n_lat)
    tn = _tile(1024, d, mult=128)
    per_seq = seq // tm
    n_parts = len(parts)
    in_specs = ([pl.BlockSpec((tm, a.shape[1]), lambda i, j: (i, 0)) for a in parts]
                + [pl.BlockSpec((w.shape[0], tn), lambda i, j: (0, j)) for w in weights]
                + [pl.BlockSpec((tm, tn), lambda i, j: (i, j)),
                   pl.BlockSpec((None, 6, tn), lambda i, j: (jnp.minimum(i // per_seq, nb), 0, j))])
    return pl.pallas_call(
        functools.partial(_res_mm_kernel, n_parts=n_parts, gate_row=gate_row),
        grid=(tt // tm, d // tn),
        in_specs=in_specs,
        out_specs=pl.BlockSpec((tm, tn), lambda i, j: (i, j)),
        out_shape=jax.ShapeDtypeStruct((tt, d), F32),
        compiler_params=_params("arbitrary", "arbitrary"),
        name="res_mm",
    )(*parts, *weights, xs, mod3)


def _softmax_sink_pv(s, sk, v):
    m = jnp.maximum(jnp.max(s, axis=-1, keepdims=True), sk)
    p = jnp.exp(s - m)
    den = jnp.sum(p, axis=-1, keepdims=True) + jnp.exp(sk - m)
    return _bdot(p, v) / den


def _sink_column(sink_ref, h, rows_per_head):
    r = lax.broadcasted_iota(jnp.int32, (A_GROUP * rows_per_head, 1), 0) // rows_per_head
    sk = jnp.zeros((A_GROUP * rows_per_head, 1), F32)
    for g in range(A_GROUP):
        sk = jnp.where(r == g, sink_ref[h * A_GROUP + g], sk)
    return sk


def _stack_heads(q):
    return jnp.concatenate([q[:, g * HEAD_DIM:(g + 1) * HEAD_DIM] for g in range(A_GROUP)], axis=0)


def _unstack_heads(o, rows):
    return jnp.concatenate([o[g * rows:(g + 1) * rows, :] for g in range(A_GROUP)], axis=1)


def _win_attn_kernel(sink_ref, q_ref, km_ref, k0_ref, kp_ref, kc_ref, vm_ref, v0_ref, vp_ref, vc_ref, o_ref, *, nblk):
    h = pl.program_id(1)
    i = pl.program_id(2)
    n_ctx = kc_ref.shape[0]
    qs = _stack_heads(q_ref[...])
    k = jnp.concatenate([km_ref[...], k0_ref[...], kp_ref[...], kc_ref[...]], axis=0)
    v = jnp.concatenate([vm_ref[...], v0_ref[...], vp_ref[...], vc_ref[...]], axis=0)
    s = _bdot_nt(qs, k) * (HEAD_DIM ** -0.5)
    nk = 3 * BLOCK + n_ctx
    r = lax.broadcasted_iota(jnp.int32, (A_GROUP * BLOCK, nk), 0) % BLOCK
    c = lax.broadcasted_iota(jnp.int32, (A_GROUP * BLOCK, nk), 1)
    band = jnp.logical_and(c >= r, c <= r + 2 * BLOCK)
    lo_ok = jnp.logical_or(c >= BLOCK, i > 0)
    hi_ok = jnp.logical_or(c < 2 * BLOCK, i < nblk - 1)
    valid = jnp.logical_or(c >= 3 * BLOCK, jnp.logical_and(band, jnp.logical_and(lo_ok, hi_ok)))
    s = jnp.where(valid, s, NEG_BIG)
    o = _softmax_sink_pv(s, _sink_column(sink_ref, h, BLOCK), v)
    o_ref[...] = _unstack_heads(o, BLOCK).astype(o_ref.dtype)


def _win_attn(p, sink, *, nbatch, seq, n_ctx):
    nblk = seq // BLOCK
    n_lat = nbatch * seq
    ctx_blk0 = n_lat // n_ctx
    kcol = A_Q_W // HEAD_DIM
    vcol = (A_Q_W + A_KV_W) // HEAD_DIM

    def kv_spec(col0, off):
        return pl.BlockSpec((BLOCK, HEAD_DIM),
                            lambda b, h, i, s: (b * nblk + jnp.clip(i + off, 0, nblk - 1), col0 + h))

    def ctx_spec(col0):
        return pl.BlockSpec((n_ctx, HEAD_DIM), lambda b, h, i, s: (ctx_blk0 + b, col0 + h))

    grid_spec = pltpu.PrefetchScalarGridSpec(
        num_scalar_prefetch=1,
        grid=(nbatch, A_KV_HEADS, nblk),
        in_specs=[pl.BlockSpec((BLOCK, A_GROUP * HEAD_DIM), lambda b, h, i, s: (b * nblk + i, h)),
                  kv_spec(kcol, -1), kv_spec(kcol, 0), kv_spec(kcol, 1), ctx_spec(kcol),
                  kv_spec(vcol, -1), kv_spec(vcol, 0), kv_spec(vcol, 1), ctx_spec(vcol)],
        out_specs=pl.BlockSpec((BLOCK, A_GROUP * HEAD_DIM), lambda b, h, i, s: (b * nblk + i, h)),
    )
    return pl.pallas_call(
        functools.partial(_win_attn_kernel, nblk=nblk),
        grid_spec=grid_spec,
        out_shape=jax.ShapeDtypeStruct((n_lat, A_Q_W), BF16),
        compiler_params=_params("arbitrary", "arbitrary", "arbitrary"),
        name="win_attn",
    )(sink, p, p, p, p, p, p, p, p, p)


def _ctx_attn_kernel(sink_ref, q_ref, k_ref, v_ref, o_ref):
    h = pl.program_id(1)
    n_ctx = q_ref.shape[0]
    s = _bdot_nt(_stack_heads(q_ref[...]), k_ref[...]) * (HEAD_DIM ** -0.5)
    o = _softmax_sink_pv(s, _sink_column(sink_ref, h, n_ctx), v_ref[...])
    o_ref[...] = _unstack_heads(o, n_ctx).astype(o_ref.dtype)


def _ctx_attn(p, sink, *, nbatch, n_lat, n_ctx):
    ctx_blk0 = n_lat // n_ctx
    kcol = A_Q_W // HEAD_DIM
    vcol = (A_Q_W + A_KV_W) // HEAD_DIM
    grid_spec = pltpu.PrefetchScalarGridSpec(
        num_scalar_prefetch=1,
        grid=(nbatch, A_KV_HEADS),
        in_specs=[pl.BlockSpec((n_ctx, A_GROUP * HEAD_DIM), lambda b, h, s: (ctx_blk0 + b, h)),
                  pl.BlockSpec((n_ctx, HEAD_DIM), lambda b, h, s: (ctx_blk0 + b, kcol + h)),
                  pl.BlockSpec((n_ctx, HEAD_DIM), lambda b, h, s: (ctx_blk0 + b, vcol + h))],
        out_specs=pl.BlockSpec((n_ctx, A_GROUP * HEAD_DIM), lambda b, h, s: (b, h)),
    )
    return pl.pallas_call(
        _ctx_attn_kernel,
        grid_spec=grid_spec,
        out_shape=jax.ShapeDtypeStruct((nbatch * n_ctx, A_Q_W), BF16),
        compiler_params=_params("arbitrary", "arbitrary"),
        name="ctx_attn",
    )(sink, p, p, p)


def _dft_tables(t):
    k = jnp.arange(t, dtype=jnp.int32)
    ang = ((k[:, None] * k[None, :]) % t).astype(F32) * (2.0 * math.pi / t)
    scale = t ** -0.5
    return jnp.cos(ang) * scale, jnp.sin(ang) * scale


def _fourier_kernel(f_ref, w1_ref, dft_ref, o_ref, stk_ref):
    t = f_ref.shape[0]

    @pl.when(pl.program_id(1) == 0)
    def _():
        f1 = jnp.dot(f_ref[...], w1_ref[...], preferred_element_type=F32)
        stk_ref[0:t, :] = f1[:, :B_W].astype(BF16)
        stk_ref[t:2 * t, :] = f1[:, B_W:].astype(BF16)

    o_ref[...] = jnp.dot(dft_ref[...], stk_ref[...], preferred_element_type=F32).astype(o_ref.dtype)


def _fourier(p, w1, dft2, *, nbatch, t, row_blk0):
    fcol = (A_Q_W + 2 * A_KV_W) // B_W
    tq = _tile(512, t)
    return pl.pallas_call(
        _fourier_kernel,
        grid=(nbatch, t // tq),
        in_specs=[pl.BlockSpec((t, B_W), lambda b, q: (row_blk0 + b, fcol)),
                  pl.BlockSpec((B_W, 2 * B_W), lambda b, q: (0, 0)),
                  pl.BlockSpec((tq, 2 * t), lambda b, q: (q, 0))],
        out_specs=pl.BlockSpec((tq, B_W), lambda b, q: (b * (t // tq) + q, 0)),
        out_shape=jax.ShapeDtypeStruct((nbatch * t, B_W), BF16),
        scratch_shapes=[pltpu.VMEM((2 * t, B_W), BF16)],
        compiler_params=_params("arbitrary", "arbitrary"),
        name="fourier",
    )(p, w1, dft2)


def _fourier_consts(t):
    cd, sd = _dft_tables(B_GROUP_DIM)
    eye = jnp.eye(B_GROUPS, dtype=F32)
    w1 = jnp.concatenate([jnp.kron(eye, cd), jnp.kron(eye, sd)], axis=1).astype(BF16)
    ct, st = _dft_tables(t)
    dft2 = jnp.concatenate([ct, -st], axis=1).astype(BF16)
    return w1, dft2


def _conv_prep_kernel(p_ref, w_ref, o_ref, pad_ref, *, tc):
    j = pl.program_id(1)
    t = p_ref.shape[0]
    half = C_CONV // 2
    pad_ref[0:8, :] = jnp.zeros((8, tc), F32)
    pad_ref[t + 8:t + 16, :] = jnp.zeros((8, tc), F32)
    pad_ref[8:t + 8, :] = p_ref[...].astype(F32)
    w = w_ref[...]
    is_q = j < (C_W // tc)
    is_qk = j < (2 * C_W // tc)
    post = jnp.where(is_q, C_DK ** -0.5, 1.0)
    rc = _tile(256, t)
    for r0 in range(0, t, rc):
        acc = w[0:1, :] * pad_ref[r0 + 8 - half:r0 + 8 - half + rc, :]
        for jj in range(1, C_CONV):
            acc = acc + w[jj:jj + 1, :] * pad_ref[r0 + 8 - half + jj:r0 + 8 - half + jj + rc, :]
        y = _silu(acc)
        for hh in range(tc // C_DK):
            sl = slice(hh * C_DK, (hh + 1) * C_DK)
            yh = y[:, sl]
            nrm = lax.rsqrt(jnp.sum(yh * yh, axis=-1, keepdims=True) + EPS) * post
            o_ref[r0:r0 + rc, sl] = (yh * jnp.where(is_qk, nrm, 1.0)).astype(o_ref.dtype)


def _conv_prep(p, conv_w, *, nseq, t, row_blk0):
    tc = 512
    ncol = 3 * C_W // tc
    return pl.pallas_call(
        functools.partial(_conv_prep_kernel, tc=tc),
        grid=(nseq, ncol),
        in_specs=[pl.BlockSpec((t, tc), lambda b, j: (row_blk0 + b, j)),
                  pl.BlockSpec((C_CONV, tc), lambda b, j: (0, j))],
        out_specs=pl.BlockSpec((t, tc), lambda b, j: (b, j)),
        out_shape=jax.ShapeDtypeStruct((nseq * t, 3 * C_W), BF16),
        scratch_shapes=[pltpu.VMEM((t + 16, tc), F32)],
        compiler_params=_params("arbitrary", "arbitrary"),
        name="conv_prep",
    )(p, conv_w)


def _delta_kernel(q_ref, k_ref, v_ref, gt_ref, alog_ref, dtb_ref, o_ref, s_ref, *, reverse, hps, d_idx):
    hg = pl.program_id(1)
    step = pl.program_id(2)
    cc = C_CHUNK

    @pl.when(step == 0)
    def _():
        s_ref[...] = jnp.zeros(s_ref.shape, F32)

    gt = gt_ref[...]
    x = gt + dtb_ref[...]
    softplus = jnp.maximum(x, 0.0) + jnp.log(1.0 + jnp.exp(-jnp.abs(x)))
    g_all = -jnp.exp(alog_ref[...]) * softplus
    beta_all = _sigmoid(gt)

    ri = lax.broadcasted_iota(jnp.int32, (cc, cc), 0)
    ci = lax.broadcasted_iota(jnp.int32, (cc, cc), 1)
    if reverse:
        incl = ci >= ri
        strict = ci > ri
        last = 0
    else:
        incl = ci <= ri
        strict = ci < ri
        last = cc - 1
    tri = jnp.where(incl, 1.0, 0.0).astype(BF16)
    g_hi, g_lo = _split(g_all)
    gc_all = jnp.dot(tri, g_hi, preferred_element_type=F32) + jnp.dot(tri, g_lo, preferred_element_type=F32)
    gc_t = gc_all.T
    eye = jnp.where(ri == ci, 1.0, 0.0)

    for hl in range(hps):
        col = d_idx * C_HEADS + hg * hps + hl
        onehot = lax.broadcasted_iota(jnp.int32, (1, GATE_W), 1) == col
        onehot_b = lax.broadcasted_iota(jnp.int32, (1, GATE_W), 1) == (col + 2 * C_HEADS)
        onehot_r = lax.broadcasted_iota(jnp.int32, (GATE_W, 1), 0) == col
        gc = jnp.sum(jnp.where(onehot, gc_all, 0.0), axis=1, keepdims=True)
        beta = jnp.sum(jnp.where(onehot_b, beta_all, 0.0), axis=1, keepdims=True)
        gc_row = jnp.sum(jnp.where(onehot_r, gc_t, 0.0), axis=0, keepdims=True)
        g_last = gc[last:last + 1, :]

        sl = slice(hl * C_DK, (hl + 1) * C_DK)
        q = q_ref[:, sl].astype(F32)
        k = k_ref[:, sl].astype(F32)
        v = v_ref[:, sl].astype(F32)
        kb = k * beta
        vb = v * beta
        decay = jnp.where(incl, jnp.exp(jnp.where(incl, gc - gc_row, 0.0)), 0.0)
        nmat = jnp.where(strict, _bdot_nt(kb, k) * decay, 0.0)
        xm = -nmat
        tm = eye + xm
        for _ in range(5):
            xm = _bdot(xm, xm)
            tm = tm + _bdot(tm, xm)
        uw = _bdot(tm, jnp.concatenate([vb, kb * jnp.exp(gc)], axis=1))
        u = uw[:, :C_DK]
        w = uw[:, C_DK:]
        a_qk = jnp.where(incl, _bdot_nt(q, k) * decay, 0.0)

        s_old = s_ref[hl]
        ws_qs = _bdot(jnp.concatenate([w, q * jnp.exp(gc)], axis=0), s_old)
        v_new = u - ws_qs[:cc, :]
        o = ws_qs[cc:, :] + _bdot(a_qk, v_new)
        k_dec = k * jnp.exp(g_last - gc)
        s_ref[hl] = s_old * jnp.exp(g_last) + _bdot_tn(k_dec, v_new)
        o_ref[:, sl] = o.astype(o_ref.dtype)


def _delta_scan(qkv, gates, a_log, dt_bias, *, reverse, nbatch, seq, n_ctx, hps=8):
    tt = qkv.shape[0]
    cc = C_CHUNK
    lc, nc = n_ctx // cc, seq // cc
    n_lat_blk = nbatch * nc
    ngrp = C_HEADS // hps
    d_idx = 1 if reverse else 0

    def row_blk(b, s):
        if reverse:
            return jnp.where(s < lc, n_lat_blk + b * lc + (lc - 1 - s), b * nc + (nc - 1 - (s - lc)))
        return jnp.where(s < lc, n_lat_blk + b * lc + s, b * nc + (s - lc))

    def col_spec(base):
        return pl.BlockSpec((cc, hps * C_DK), lambda b, g, s: (row_blk(b, s), base * ngrp + g))

    pad = jnp.zeros((1, GATE_W - 2 * C_HEADS), F32)
    alog_row = jnp.concatenate([a_log.reshape(1, 2 * C_HEADS), pad], axis=1)
    dtb_row = jnp.concatenate([dt_bias.reshape(1, 2 * C_HEADS), pad], axis=1)
    return pl.pallas_call(
        functools.partial(_delta_kernel, reverse=reverse, hps=hps, d_idx=d_idx),
        grid=(nbatch, ngrp, lc + nc),
        in_specs=[col_spec(0), col_spec(1), col_spec(2),
                  pl.BlockSpec((cc, GATE_W), lambda b, g, s: (row_blk(b, s), 0)),
                  pl.BlockSpec((1, GATE_W), lambda b, g, s: (0, 0)),
                  pl.BlockSpec((1, GATE_W), lambda b, g, s: (0, 0))],
        out_specs=pl.BlockSpec((cc, hps * C_DK), lambda b, g, s: (row_blk(b, s), g)),
        out_shape=jax.ShapeDtypeStruct((tt, C_W), BF16),
        scratch_shapes=[pltpu.VMEM((hps, C_DK, C_DK), F32)],
        compiler_params=_params("arbitrary", "arbitrary", "arbitrary"),
        name="delta_bwd" if reverse else "delta_fwd",
    )(qkv, qkv, qkv, gates, alog_row, dtb_row)


def _gated_norm_kernel(of_ref, ob_ref, z_ref, nw_ref, o_ref):
    nw = nw_ref[...]
    for h in range(C_HEADS):
        sl = slice(h * C_DK, (h + 1) * C_DK)
        o = of_ref[:, sl].astype(F32) + ob_ref[:, sl].astype(F32)
        on = o * lax.rsqrt(jnp.mean(o * o, axis=-1, keepdims=True) + EPS) * nw
        o_ref[:, sl] = (on * _silu(z_ref[:, sl].astype(F32))).astype(o_ref.dtype)


def _gated_norm(o_f, o_b, p, norm_w):
    tt = o_f.shape[0]
    tm = _tile(512, tt)
    zcol = 3 * C_W // C_W
    return pl.pallas_call(
        _gated_norm_kernel,
        grid=(tt // tm,),
        in_specs=[pl.BlockSpec((tm, C_W), lambda i: (i, 0)),
                  pl.BlockSpec((tm, C_W), lambda i: (i, 0)),
                  pl.BlockSpec((tm, C_W), lambda i: (i, zcol)),
                  pl.BlockSpec((1, C_DK), lambda i: (0, 0))],
        out_specs=pl.BlockSpec((tm, C_W), lambda i: (i, 0)),
        out_shape=jax.ShapeDtypeStruct((tt, C_W), BF16),
        compiler_params=_params("arbitrary"),
        name="gated_norm",
    )(o_f, o_b, p, norm_w.reshape(1, C_DK))


def _max_first(vals):
    bv = vals[0]
    bi = jnp.zeros(vals[0].shape, jnp.int32)
    for idx in range(1, len(vals)):
        upd = vals[idx] > bv
        bi = jnp.where(upd, idx, bi)
        bv = jnp.where(upd, vals[idx], bv)
    return bv, bi


def _pick(vals, idx):
    out = vals[0]
    for n in range(1, len(vals)):
        out = jnp.where(idx == n, vals[n], out)
    return out


def _router_kernel(x_ref, mod_ref, nw_ref, rwt_ref, rb_ref, v_ref, route_ref):
    v = _norm_mod(x_ref[...], nw_ref[...], mod_ref[...], 3, 4)
    v_ref[...] = v
    vh, vl = _split(v)
    wh, wl = _split(rwt_ref[...])
    logits = _bdot_nt(wh, vh) + _bdot_nt(wh, vl) + _bdot_nt(wl, vh)
    scores = _sigmoid(logits)
    sel = scores + rb_ref[...]
    sel_r = [sel[e:e + 1, :] for e in range(N_EXPERTS)]
    sc_r = [scores[e:e + 1, :] for e in range(N_EXPERTS)]
    epg = EXPERTS_PER_GROUP
    gscore = []
    for g in range(N_GROUPS):
        a, b, c, d = sel_r[g * epg:(g + 1) * epg]
        m_ab, n_ab = jnp.maximum(a, b), jnp.minimum(a, b)
        m_cd, n_cd = jnp.maximum(c, d), jnp.minimum(c, d)
        top1 = jnp.maximum(m_ab, m_cd)
        top2 = jnp.maximum(jnp.minimum(m_ab, m_cd), jnp.maximum(n_ab, n_cd))
        gscore.append(top1 + top2)
    _, best = _max_first(gscore)
    in_sel = [_pick([sel_r[g * epg + k] for g in range(N_GROUPS)], best) for k in range(epg)]
    in_sc = [_pick([sc_r[g * epg + k] for g in range(N_GROUPS)], best) for k in range(epg)]
    _, i1 = _max_first(in_sel)
    _, i2 = _max_first([jnp.where(i1 == k, -jnp.inf, in_sel[k]) for k in range(epg)])
    s1 = _pick(in_sc, i1)
    s2 = _pick(in_sc, i2)
    tot = s1 + s2
    rows = [(best * epg + i1).astype(F32), (best * epg + i2).astype(F32), s1 / tot, s2 / tot]
    rows += [jnp.zeros_like(s1)] * 4
    route_ref[...] = jnp.concatenate(rows, axis=0)


def _router(xs, mod3, nw, router_w, router_b, *, n_lat, seq):
    tt, d = xs.shape
    nb = mod3.shape[0] - 1
    tm = _tile(256, seq, tt - n_lat, n_lat)
    per_seq = seq // tm
    nt = tt // tm
    return pl.pallas_call(
        _router_kernel,
        grid=(nt,),
        in_specs=[pl.BlockSpec((tm, d), lambda i: (i, 0)),
                  pl.BlockSpec((None, 6, d), lambda i: (jnp.minimum(i // per_seq, nb), 0, 0)),
                  pl.BlockSpec((1, d), lambda i: (0, 0)),
                  pl.BlockSpec((N_EXPERTS, d), lambda i: (0, 0)),
                  pl.BlockSpec((N_EXPERTS, 1), lambda i: (0, 0))],
        out_specs=[pl.BlockSpec((tm, d), lambda i: (i, 0)),
                   pl.BlockSpec((None, 8, tm), lambda i: (i, 0, 0))],
        out_shape=[jax.ShapeDtypeStruct((tt, d), F32),
                   jax.ShapeDtypeStruct((nt, 8, tm), F32)],
        compiler_params=_params("arbitrary"),
        name="router",
    )(xs, mod3, nw, router_w.T, router_b.reshape(N_EXPERTS, 1))


def _load_tile_indices(dest_hbm, idx_smem, sem, n):
    cp = pltpu.make_async_copy(dest_hbm.at[pl.ds(pl.program_id(0) * n, n)], idx_smem, sem)
    cp.start()
    cp.wait()


def _dispatch_kernel(dest_hbm, v_ref, init_hbm, xs_hbm, idx_smem, sem_idx, sem, *, tm):
    del init_hbm
    _load_tile_indices(dest_hbm, idx_smem, sem_idx, 2 * tm)

    def row_copy(t, d):
        return pltpu.make_async_copy(v_ref.at[pl.ds(t, 1)], xs_hbm.at[pl.ds(d, 1)], sem)

    def issue(t, c):
        row_copy(t, idx_smem[t]).start()
        row_copy(t, idx_smem[tm + t]).start()
        return c

    def drain(t, c):
        row_copy(t, idx_smem[t]).wait()
        row_copy(t, idx_smem[tm + t]).wait()
        return c

    lax.fori_loop(0, tm, issue, 0)
    lax.fori_loop(0, tm, drain, 0)


def _dispatch(dest_tiles, v, xs_init, *, tm):
    tt, d = v.shape
    return pl.pallas_call(
        functools.partial(_dispatch_kernel, tm=tm),
        grid=(tt // tm,),
        in_specs=[pl.BlockSpec(memory_space=pl.ANY),
                  pl.BlockSpec((tm, d), lambda i: (i, 0)),
                  pl.BlockSpec(memory_space=pl.ANY)],
        out_specs=pl.BlockSpec(memory_space=pl.ANY),
        out_shape=jax.ShapeDtypeStruct(xs_init.shape, F32),
        scratch_shapes=[pltpu.SMEM((2 * tm,), jnp.int32), pltpu.SemaphoreType.DMA(()), pltpu.SemaphoreType.DMA(())],
        input_output_aliases={2: 0},
        compiler_params=_params("arbitrary"),
        name="moe_dispatch",
    )(dest_tiles, v, xs_init)


def _expert_kernel(te_ref, tv_ref, x_ref, wg_ref, wu_ref, wd_ref, o_ref):
    i = pl.program_id(0)

    @pl.when(tv_ref[i] > 0)
    def _():
        x = x_ref[...].astype(BF16)
        hg = jnp.dot(x, wg_ref[...], preferred_element_type=F32)
        hu = jnp.dot(x, wu_ref[...], preferred_element_type=F32)
        act = (_silu(hg) * hu).astype(BF16)
        o_ref[...] = jnp.dot(act, wd_ref[...], preferred_element_type=F32)

    @pl.when(tv_ref[i] == 0)
    def _():
        o_ref[...] = jnp.zeros(o_ref.shape, F32)


def _experts(tile_expert, tile_valid, xs, wg, wu, wd, *, tm):
    rows, d = xs.shape
    dff = wg.shape[2]
    grid_spec = pltpu.PrefetchScalarGridSpec(
        num_scalar_prefetch=2,
        grid=(rows // tm,),
        in_specs=[pl.BlockSpec((tm, d), lambda i, te, tv: (i, 0)),
                  pl.BlockSpec((None, d, dff), lambda i, te, tv: (te[i], 0, 0)),
                  pl.BlockSpec((None, d, dff), lambda i, te, tv: (te[i], 0, 0)),
                  pl.BlockSpec((None, dff, d), lambda i, te, tv: (te[i], 0, 0))],
        out_specs=pl.BlockSpec((tm, d), lambda i, te, tv: (i, 0)),
    )
    return pl.pallas_call(
        _expert_kernel,
        grid_spec=grid_spec,
        out_shape=jax.ShapeDtypeStruct((rows, d), F32),
        compiler_params=_params("arbitrary"),
        name="moe_experts",
    )(tile_expert, tile_valid, xs, wg, wu, wd)


def _combine_kernel(dest_hbm, ys_hbm, x_ref, w_ref, mod_ref, o_ref, idx_smem, ya_ref, yb_ref, sem_idx, sem, *, tm):
    _load_tile_indices(dest_hbm, idx_smem, sem_idx, 2 * tm)

    def row_copy(d, dst_ref, t):
        return pltpu.make_async_copy(ys_hbm.at[pl.ds(d, 1)], dst_ref.at[pl.ds(t, 1)], sem)

    def issue(t, c):
        row_copy(idx_smem[t], ya_ref, t).start()
        row_copy(idx_smem[tm + t], yb_ref, t).start()
        return c

    def drain(t, c):
        row_copy(idx_smem[t], ya_ref, t).wait()
        row_copy(idx_smem[tm + t], yb_ref, t).wait()
        return c

    lax.fori_loop(0, tm, issue, 0)
    lax.fori_loop(0, tm, drain, 0)
    w = w_ref[...]
    ff = w[:, 0:1] * ya_ref[...] + w[:, 1:2] * yb_ref[...]
    o_ref[...] = x_ref[...] + mod_ref[5:6, :] * ff


def _combine(dest_tiles, ys, xs, w12, mod3, *, tm, n_lat, seq):
    tt, d = xs.shape
    nb = mod3.shape[0] - 1
    per_seq = seq // tm
    return pl.pallas_call(
        functools.partial(_combine_kernel, tm=tm),
        grid=(tt // tm,),
        in_specs=[pl.BlockSpec(memory_space=pl.ANY),
                  pl.BlockSpec(memory_space=pl.ANY),
                  pl.BlockSpec((tm, d), lambda i: (i, 0)),
                  pl.BlockSpec((tm, 2), lambda i: (i, 0)),
                  pl.BlockSpec((None, 6, d), lambda i: (jnp.minimum(i // per_seq, nb), 0, 0))],
        out_specs=pl.BlockSpec((tm, d), lambda i: (i, 0)),
        out_shape=jax.ShapeDtypeStruct((tt, d), F32),
        scratch_shapes=[pltpu.SMEM((2 * tm,), jnp.int32), pltpu.VMEM((tm, d), F32), pltpu.VMEM((tm, d), F32),
                        pltpu.SemaphoreType.DMA(()), pltpu.SemaphoreType.DMA(())],
        compiler_params=_params("arbitrary"),
        name="moe_combine",
    )(dest_tiles, ys, xs, w12, mod3)


def _moe(xs, mod3, nw, router_w, router_b, wg, wu, wd, *, n_lat, seq):
    tt, d = xs.shape
    v, route = _router(xs, mod3, nw, router_w, router_b, n_lat=n_lat, seq=seq)
    route = route.transpose(1, 0, 2).reshape(8, tt)
    e12 = route[0:2].astype(jnp.int32)
    w12 = route[2:4].T

    tme = _tile(256, tt)
    flat_e = e12.reshape(-1)
    onehot = (flat_e[:, None] == jnp.arange(N_EXPERTS, dtype=jnp.int32)[None, :]).astype(jnp.int32)
    csum = jnp.cumsum(onehot, axis=0)
    rank = jnp.sum(onehot * csum, axis=1) - 1
    counts = csum[-1]
    padded = ((counts + tme - 1) // tme) * tme
    ends = jnp.cumsum(padded)
    starts = ends - padded
    dest = (jnp.sum(onehot * starts[None, :], axis=1) + rank).reshape(2, tt)
    n_tiles = (2 * tt) // tme + N_EXPERTS
    tile_row0 = jnp.arange(n_tiles, dtype=jnp.int32) * tme
    tile_expert = jnp.minimum(jnp.sum((tile_row0[:, None] >= ends[None, :]).astype(jnp.int32), axis=1), N_EXPERTS - 1)
    tile_valid = (tile_row0 < ends[-1]).astype(jnp.int32)

    tmd = _tile(512, seq, tt - n_lat, n_lat)
    dest_tiles = dest.reshape(2, tt // tmd, tmd).transpose(1, 0, 2).reshape(-1)
    xs_sorted = _dispatch(dest_tiles, v, jnp.zeros((n_tiles * tme, d), F32), tm=tmd)
    ys = _experts(tile_expert, tile_valid, xs_sorted, wg, wu, wd, tm=tme)
    return _combine(dest_tiles, ys, xs, w12, mod3, tm=tmd, n_lat=n_lat, seq=seq)


def _final_norm_kernel(x_ref, nw_ref, o_ref):
    x = x_ref[...]
    o_ref[...] = x * lax.rsqrt(jnp.mean(x * x, axis=-1, keepdims=True) + EPS) * nw_ref[...]


def _final_norm(xs, nw, *, n_lat):
    d = xs.shape[1]
    tm = _tile(512, n_lat)
    return pl.pallas_call(
        _final_norm_kernel,
        grid=(n_lat // tm,),
        in_specs=[pl.BlockSpec((tm, d), lambda i: (i, 0)), pl.BlockSpec((1, d), lambda i: (0, 0))],
        out_specs=pl.BlockSpec((tm, d), lambda i: (i, 0)),
        out_shape=jax.ShapeDtypeStruct((n_lat, d), F32),
        compiler_params=_params("arbitrary"),
        name="final_norm",
    )(xs, nw.reshape(1, d))


def _rope_tables(seq):
    rows = seq // GRID_W
    r = jnp.repeat(jnp.arange(rows, dtype=F32), GRID_W)
    col = jnp.tile(jnp.arange(GRID_W, dtype=F32), rows)
    n_freq = HEAD_DIM // 4
    inv = ROPE_BASE ** (-jnp.arange(n_freq, dtype=F32) / n_freq)
    ar, ac = r[:, None] * inv, col[:, None] * inv
    cos_t = jnp.concatenate([jnp.cos(ar), jnp.cos(ar), jnp.cos(ac), jnp.cos(ac)], axis=1)
    sin_t = jnp.concatenate([-jnp.sin(ar), jnp.sin(ar), -jnp.sin(ac), jnp.sin(ac)], axis=1)
    return cos_t, sin_t


def _even_layer(xs, mod3, nw, w_in, w_out, sink, rope, fconst_lat, fconst_ctx, *, nbatch, seq, n_ctx):
    n_lat = nbatch * seq
    cos_t, sin_t = rope
    p = _norm_mm(xs, mod3, nw, w_in.astype(BF16), cos_t, sin_t, n_lat=n_lat, seq=seq, rope_cols=A_Q_W + A_KV_W)
    a_lat = _win_attn(p, sink, nbatch=nbatch, seq=seq, n_ctx=n_ctx)
    a_ctx = _ctx_attn(p, sink, nbatch=nbatch, n_lat=n_lat, n_ctx=n_ctx)
    f_lat = _fourier(p, *fconst_lat, nbatch=nbatch, t=seq, row_blk0=0)
    f_ctx = _fourier(p, *fconst_ctx, nbatch=nbatch, t=n_ctx, row_blk0=n_lat // n_ctx)
    a = jnp.concatenate([a_lat, a_ctx], axis=0)
    f = jnp.concatenate([f_lat, f_ctx], axis=0)
    wo = w_out.astype(BF16)
    return _res_mm([a, f], [wo[:A_Q_W], wo[A_Q_W:]], xs, mod3, gate_row=2, n_lat=n_lat, seq=seq)


def _odd_layer(xs, mod3, nw, w_in, conv_w, a_log, dt_bias, norm_w, w_out, rope, *, nbatch, seq, n_ctx):
    n_lat = nbatch * seq
    d = xs.shape[1]
    cos_t, sin_t = rope
    n_main = 4 * C_W
    w_main = w_in[:, :n_main].astype(BF16)
    w_gate = jnp.concatenate([w_in[:, n_main:], jnp.zeros((d, GATE_W - 4 * C_HEADS), F32)], axis=1).astype(BF16)
    p = _norm_mm(xs, mod3, nw, w_main, cos_t, sin_t, n_lat=n_lat, seq=seq, rope_cols=0)
    gates = _norm_mm(xs, mod3, nw, w_gate, cos_t, sin_t, n_lat=n_lat, seq=seq, rope_cols=0, out_dtype=F32)
    qkv_lat = _conv_prep(p, conv_w, nseq=nbatch, t=seq, row_blk0=0)
    qkv_ctx = _conv_prep(p, conv_w, nseq=nbatch, t=n_ctx, row_blk0=n_lat // n_ctx)
    qkv = jnp.concatenate([qkv_lat, qkv_ctx], axis=0)
    o_f = _delta_scan(qkv, gates, a_log, dt_bias, reverse=False, nbatch=nbatch, seq=seq, n_ctx=n_ctx)
    o_b = _delta_scan(qkv, gates, a_log, dt_bias, reverse=True, nbatch=nbatch, seq=seq, n_ctx=n_ctx)
    on = _gated_norm(o_f, o_b, p, norm_w)
    return _res_mm([on], [w_out.astype(BF16)], xs, mod3, gate_row=2, n_lat=n_lat, seq=seq)


def kernel(x, c, ctx, c_ctx, adaln_w, adaln_b, norm_mix_w, norm_ffn_w, attn_in_w, attn_out_w, attn_sink, dn_in_w, dn_conv_w, dn_a_log, dn_dt_bias, dn_norm_w, dn_out_w, router_w, router_b, exp_gate_w, exp_up_w, exp_down_w, final_norm_w):
    nbatch, seq, d = x.shape
    n_ctx = ctx.shape[1]
    depth = adaln_w.shape[0]
    n_lat = nbatch * seq
    assert seq % BLOCK == 0 and seq % GRID_W == 0 and n_ctx % C_CHUNK == 0 and n_lat % n_ctx == 0

    xs = jnp.concatenate([x.reshape(n_lat, d), ctx.reshape(nbatch * n_ctx, d)], axis=0)
    mod_rows = 8 * ((nbatch + 1 + 7) // 8)
    c_all = jnp.concatenate([c, c_ctx[None, :], jnp.zeros((mod_rows - nbatch - 1, d), F32)], axis=0)
    mod_all = _adaln(c_all, adaln_w, adaln_b)[:, :nbatch + 1].reshape(depth, nbatch + 1, 6, d)

    rope = _rope_tables(seq)
    fconst_lat = _fourier_consts(seq)
    fconst_ctx = _fourier_consts(n_ctx)

    for layer in range(depth):
        mod3 = mod_all[layer]
        i = layer // 2
        nw = norm_mix_w[layer].reshape(1, d)
        if layer % 2 == 0:
            xs = _even_layer(xs, mod3, nw, attn_in_w[i], attn_out_w[i], attn_sink[i], rope, fconst_lat, fconst_ctx,
                             nbatch=nbatch, seq=seq, n_ctx=n_ctx)
        else:
            xs = _odd_layer(xs, mod3, nw, dn_in_w[i], dn_conv_w[i], dn_a_log[i], dn_dt_bias[i], dn_norm_w[i],
                            dn_out_w[i], rope, nbatch=nbatch, seq=seq, n_ctx=n_ctx)
        xs = _moe(xs, mod3, norm_ffn_w[layer].reshape(1, d), router_w, router_b,
                  exp_gate_w[layer].astype(BF16), exp_up_w[layer].astype(BF16), exp_down_w[layer].astype(BF16),
                  n_lat=n_lat, seq=seq)
    return _final_norm(xs, final_norm_w, n_lat=n_lat).reshape(nbatch, seq, d)
```

```python
import functools
import math

import jax
import jax.numpy as jnp
from jax import lax
from jax.experimental import pallas as pl
from jax.experimental.pallas import tpu as pltpu

F32 = jnp.float32
BF16 = jnp.bfloat16
EPS = 1e-6

HEAD_DIM = 128
A_Q_HEADS = 12
A_KV_HEADS = 4
A_GROUP = A_Q_HEADS // A_KV_HEADS
BLOCK = 128
GRID_W = 64
ROPE_BASE = 10000.0
B_GROUPS = 4
B_GROUP_DIM = 128
C_HEADS = 16
C_DK = 128
C_CONV = 5
C_CHUNK = 64
N_EXPERTS = 16
N_GROUPS = 4
EXPERTS_PER_GROUP = N_EXPERTS // N_GROUPS

A_Q_W = A_Q_HEADS * HEAD_DIM
A_KV_W = A_KV_HEADS * HEAD_DIM
B_W = B_GROUPS * B_GROUP_DIM
C_W = C_HEADS * C_DK
GATE_W = 128

V7X_VMEM_LIMIT_BYTES = 56 * 1024 * 1024
NEG_BIG = -1e30


def _tile(pref, *dims, mult=8):
    t = min((pref,) + dims)
    t -= t % mult
    while t > mult and any(d % t for d in dims):
        t -= mult
    assert t >= mult and all(d % t == 0 for d in dims), (pref, dims)
    return t


def _params(*sem):
    return pltpu.CompilerParams(dimension_semantics=sem, vmem_limit_bytes=V7X_VMEM_LIMIT_BYTES)


def _bdot(a, b):
    return jnp.dot(a.astype(BF16), b.astype(BF16), preferred_element_type=F32)


def _bdot_nt(a, b):
    return lax.dot_general(a.astype(BF16), b.astype(BF16), (((1,), (1,)), ((), ())), preferred_element_type=F32)


def _bdot_tn(a, b):
    return lax.dot_general(a.astype(BF16), b.astype(BF16), (((0,), (0,)), ((), ())), preferred_element_type=F32)


def _split(a):
    hi = a.astype(BF16)
    lo = (a - hi.astype(F32)).astype(BF16)
    return hi, lo


def _sigmoid(x):
    return 1.0 / (1.0 + jnp.exp(-x))


def _silu(x):
    return x * _sigmoid(x)


def _adaln_kernel(c_ref, w_ref, b_ref, o_ref):
    sc = _silu(c_ref[...])
    sh, sl = _split(sc)
    wh, wl = _split(w_ref[...])
    acc = (jnp.dot(sh, wh, preferred_element_type=F32) + jnp.dot(sh, wl, preferred_element_type=F32)
           + jnp.dot(sl, wh, preferred_element_type=F32))
    o_ref[...] = acc + b_ref[...]


def _adaln(c_all, adaln_w, adaln_b):
    depth, d, n6 = adaln_w.shape
    rows = c_all.shape[0]
    tn = _tile(512, n6, mult=128)
    return pl.pallas_call(
        _adaln_kernel,
        grid=(depth, n6 // tn),
        in_specs=[pl.BlockSpec((rows, d), lambda l, j: (0, 0)),
                  pl.BlockSpec((None, d, tn), lambda l, j: (l, 0, j)),
                  pl.BlockSpec((None, 1, tn), lambda l, j: (l, 0, j))],
        out_specs=pl.BlockSpec((None, rows, tn), lambda l, j: (l, 0, j)),
        out_shape=jax.ShapeDtypeStruct((depth, rows, n6), F32),
        compiler_params=_params("arbitrary", "arbitrary"),
        name="adaln",
    )(c_all, adaln_w, adaln_b.reshape(depth, 1, n6))


def _norm_mod(x, nw, mod, shift_row, scale_row):
    ms = jnp.mean(x * x, axis=-1, keepdims=True)
    y = x * lax.rsqrt(ms + EPS) * nw
    return y * (1.0 + mod[scale_row:scale_row + 1, :]) + mod[shift_row:shift_row + 1, :]


def _rope_slab(a, cos, sin):
    lane = lax.broadcasted_iota(jnp.int32, a.shape, 1)
    first = (lane % 64) < 32
    partner = jnp.where(first, pltpu.roll(a, 96, 1), pltpu.roll(a, 32, 1))
    return a * cos + partner * sin


def _norm_mm_kernel(x_ref, mod_ref, nw_ref, w_ref, cos_ref, sin_ref, o_ref, u_ref, *, rope_tiles, n_lat_tiles, tn):
    i = pl.program_id(0)
    j = pl.program_id(1)

    @pl.when(j == 0)
    def _():
        u_ref[...] = _norm_mod(x_ref[...], nw_ref[...], mod_ref[...], 0, 1).astype(BF16)

    acc = jnp.dot(u_ref[...], w_ref[...], preferred_element_type=F32)
    if rope_tiles == 0:
        o_ref[...] = acc.astype(o_ref.dtype)
    else:
        do_rope = jnp.logical_and(j < rope_tiles, i < n_lat_tiles)

        @pl.when(do_rope)
        def _():
            cos = cos_ref[...]
            sin = sin_ref[...]
            for s in range(tn // HEAD_DIM):
                sl = slice(s * HEAD_DIM, (s + 1) * HEAD_DIM)
                o_ref[:, sl] = _rope_slab(acc[:, sl], cos, sin).astype(o_ref.dtype)

        @pl.when(jnp.logical_not(do_rope))
        def _():
            o_ref[...] = acc.astype(o_ref.dtype)


def _norm_mm(xs, mod3, nw, w, cos_t, sin_t, *, n_lat, seq, rope_cols, out_dtype=BF16, tn_pref=1024):
    tt, d = xs.shape
    nout = w.shape[1]
    nb = mod3.shape[0] - 1
    tm = _tile(512, seq, tt - n_lat, n_lat)
    tn = _tile(tn_pref, nout, *( (rope_cols,) if rope_cols else ()), mult=128)
    per_seq = seq // tm
    kern = functools.partial(_norm_mm_kernel, rope_tiles=rope_cols // tn, n_lat_tiles=n_lat // tm, tn=tn)
    return pl.pallas_call(
        kern,
        grid=(tt // tm, nout // tn),
        in_specs=[pl.BlockSpec((tm, d), lambda i, j: (i, 0)),
                  pl.BlockSpec((None, 6, d), lambda i, j: (jnp.minimum(i // per_seq, nb), 0, 0)),
                  pl.BlockSpec((1, d), lambda i, j: (0, 0)),
                  pl.BlockSpec((d, tn), lambda i, j: (0, j)),
                  pl.BlockSpec((tm, HEAD_DIM), lambda i, j: (i % per_seq, 0)),
                  pl.BlockSpec((tm, HEAD_DIM), lambda i, j: (i % per_seq, 0))],
        out_specs=pl.BlockSpec((tm, tn), lambda i, j: (i, j)),
        out_shape=jax.ShapeDtypeStruct((tt, nout), out_dtype),
        scratch_shapes=[pltpu.VMEM((tm, d), BF16)],
        compiler_params=_params("arbitrary", "arbitrary"),
        name="norm_mm",
    )(xs, mod3, nw, w, cos_t, sin_t)


def _res_mm_kernel(*refs, n_parts, gate_row):
    a_refs = refs[:n_parts]
    w_refs = refs[n_parts:2 * n_parts]
    x_ref, mod_ref, o_ref = refs[2 * n_parts:]
    acc = jnp.dot(a_refs[0][...], w_refs[0][...], preferred_element_type=F32)
    for a_ref, w_ref in zip(a_refs[1:], w_refs[1:]):
        acc = acc + jnp.dot(a_ref[...], w_ref[...], preferred_element_type=F32)
    o_ref[...] = x_ref[...] + mod_ref[gate_row:gate_row + 1, :] * acc


def _res_mm(parts, weights, xs, mod3, *, gate_row, n_lat, seq):
    tt, d = xs.shape
    nb = mod3.shape[0] - 1
    tm = _tile(512, seq, tt - n_lat, n_lat)
    tn = _tile(1024, d, mult=128)
    per_seq = seq // tm
    n_parts = len(parts)
    in_specs = ([pl.BlockSpec((tm, a.shape[1]), lambda i, j: (i, 0)) for a in parts]
                + [pl.BlockSpec((w.shape[0], tn), lambda i, j: (0, j)) for w in weights]
                + [pl.BlockSpec((tm, tn), lambda i, j: (i, j)),
                   pl.BlockSpec((None, 6, tn), lambda i, j: (jnp.minimum(i // per_seq, nb), 0, j))])
    return pl.pallas_call(
        functools.partial(_res_mm_kernel, n_parts=n_parts, gate_row=gate_row),
        grid=(tt // tm, d // tn),
        in_specs=in_specs,
        out_specs=pl.BlockSpec((tm, tn), lambda i, j: (i, j)),
        out_shape=jax.ShapeDtypeStruct((tt, d), F32),
        compiler_params=_params("arbitrary", "arbitrary"),
        name="res_mm",
    )(*parts, *weights, xs, mod3)


def _softmax_sink_pv(s, sk, v):
    m = jnp.maximum(jnp.max(s, axis=-1, keepdims=True), sk)
    p = jnp.exp(s - m)
    den = jnp.sum(p, axis=-1, keepdims=True) + jnp.exp(sk - m)
    return _bdot(p, v) / den


def _sink_column(sink_ref, h, rows_per_head):
    r = lax.broadcasted_iota(jnp.int32, (A_GROUP * rows_per_head, 1), 0) // rows_per_head
    sk = jnp.zeros((A_GROUP * rows_per_head, 1), F32)
    for g in range(A_GROUP):
        sk = jnp.where(r == g, sink_ref[h * A_GROUP + g], sk)
    return sk


def _stack_heads(q):
    return jnp.concatenate([q[:, g * HEAD_DIM:(g + 1) * HEAD_DIM] for g in range(A_GROUP)], axis=0)


def _unstack_heads(o, rows):
    return jnp.concatenate([o[g * rows:(g + 1) * rows, :] for g in range(A_GROUP)], axis=1)


def _win_attn_kernel(sink_ref, q_ref, km_ref, k0_ref, kp_ref, kc_ref, vm_ref, v0_ref, vp_ref, vc_ref, o_ref, *, nblk):
    i = pl.program_id(1)
    n_ctx = kc_ref.shape[0]
    nk = 3 * BLOCK + n_ctx
    r = lax.broadcasted_iota(jnp.int32, (A_GROUP * BLOCK, nk), 0) % BLOCK
    c = lax.broadcasted_iota(jnp.int32, (A_GROUP * BLOCK, nk), 1)
    band = jnp.logical_and(c >= r, c <= r + 2 * BLOCK)
    lo_ok = jnp.logical_or(c >= BLOCK, i > 0)
    hi_ok = jnp.logical_or(c < 2 * BLOCK, i < nblk - 1)
    valid = jnp.logical_or(c >= 3 * BLOCK, jnp.logical_and(band, jnp.logical_and(lo_ok, hi_ok)))
    for h in range(A_KV_HEADS):
        hs = slice(h * HEAD_DIM, (h + 1) * HEAD_DIM)
        qsl = slice(h * A_GROUP * HEAD_DIM, (h + 1) * A_GROUP * HEAD_DIM)
        qs = _stack_heads(q_ref[:, qsl])
        k = jnp.concatenate([km_ref[:, hs], k0_ref[:, hs], kp_ref[:, hs], kc_ref[:, hs]], axis=0)
        v = jnp.concatenate([vm_ref[:, hs], v0_ref[:, hs], vp_ref[:, hs], vc_ref[:, hs]], axis=0)
        s = jnp.where(valid, _bdot_nt(qs, k) * (HEAD_DIM ** -0.5), NEG_BIG)
        o = _softmax_sink_pv(s, _sink_column(sink_ref, h, BLOCK), v)
        o_ref[:, qsl] = _unstack_heads(o, BLOCK).astype(o_ref.dtype)


def _win_attn(p, sink, *, nbatch, seq, n_ctx):
    tt = p.shape[0]
    nblk = seq // BLOCK
    n_lat = nbatch * seq
    ctx_blk0 = n_lat // n_ctx
    kcol = A_Q_W // A_KV_W
    vcol = (A_Q_W + A_KV_W) // A_KV_W

    def kv_spec(col, off):
        return pl.BlockSpec((BLOCK, A_KV_W), lambda b, i, s: (b * nblk + jnp.clip(i + off, 0, nblk - 1), col))

    def ctx_spec(col):
        return pl.BlockSpec((n_ctx, A_KV_W), lambda b, i, s: (ctx_blk0 + b, col))

    grid_spec = pltpu.PrefetchScalarGridSpec(
        num_scalar_prefetch=1,
        grid=(nbatch, nblk),
        in_specs=[pl.BlockSpec((BLOCK, A_Q_W), lambda b, i, s: (b * nblk + i, 0)),
                  kv_spec(kcol, -1), kv_spec(kcol, 0), kv_spec(kcol, 1), ctx_spec(kcol),
                  kv_spec(vcol, -1), kv_spec(vcol, 0), kv_spec(vcol, 1), ctx_spec(vcol)],
        out_specs=pl.BlockSpec((BLOCK, A_Q_W), lambda b, i, s: (b * nblk + i, 0)),
    )
    return pl.pallas_call(
        functools.partial(_win_attn_kernel, nblk=nblk),
        grid_spec=grid_spec,
        out_shape=jax.ShapeDtypeStruct((tt, A_Q_W), BF16),
        compiler_params=_params("arbitrary", "arbitrary"),
        name="win_attn",
    )(sink, p, p, p, p, p, p, p, p, p)


def _ctx_attn_kernel(sink_ref, q_ref, k_ref, v_ref, prev_ref, o_ref):
    del prev_ref
    h = pl.program_id(1)
    n_ctx = q_ref.shape[0]
    s = _bdot_nt(_stack_heads(q_ref[...]), k_ref[...]) * (HEAD_DIM ** -0.5)
    o = _softmax_sink_pv(s, _sink_column(sink_ref, h, n_ctx), v_ref[...])
    o_ref[...] = _unstack_heads(o, n_ctx).astype(o_ref.dtype)


def _ctx_attn(p, sink, a_lat, *, nbatch, n_lat, n_ctx):
    ctx_blk0 = n_lat // n_ctx
    kcol = A_Q_W // HEAD_DIM
    vcol = (A_Q_W + A_KV_W) // HEAD_DIM
    grid_spec = pltpu.PrefetchScalarGridSpec(
        num_scalar_prefetch=1,
        grid=(nbatch, A_KV_HEADS),
        in_specs=[pl.BlockSpec((n_ctx, A_GROUP * HEAD_DIM), lambda b, h, s: (ctx_blk0 + b, h)),
                  pl.BlockSpec((n_ctx, HEAD_DIM), lambda b, h, s: (ctx_blk0 + b, kcol + h)),
                  pl.BlockSpec((n_ctx, HEAD_DIM), lambda b, h, s: (ctx_blk0 + b, vcol + h)),
                  pl.BlockSpec(memory_space=pl.ANY)],
        out_specs=pl.BlockSpec((n_ctx, A_GROUP * HEAD_DIM), lambda b, h, s: (ctx_blk0 + b, h)),
    )
    return pl.pallas_call(
        _ctx_attn_kernel,
        grid_spec=grid_spec,
        out_shape=jax.ShapeDtypeStruct(a_lat.shape, BF16),
        input_output_aliases={4: 0},
        compiler_params=_params("arbitrary", "arbitrary"),
        name="ctx_attn",
    )(sink, p, p, p, a_lat)


def _dft_tables(t):
    k = jnp.arange(t, dtype=jnp.int32)
    ang = ((k[:, None] * k[None, :]) % t).astype(F32) * (2.0 * math.pi / t)
    scale = t ** -0.5
    return jnp.cos(ang) * scale, jnp.sin(ang) * scale


def _fourier_kernel(f_ref, w1_ref, dft_ref, *rest):
    o_ref, stk_ref = rest[-2:]
    t = f_ref.shape[0]

    @pl.when(pl.program_id(1) == 0)
    def _():
        f1 = jnp.dot(f_ref[...], w1_ref[...], preferred_element_type=F32)
        stk_ref[0:t, :] = f1[:, :B_W].astype(BF16)
        stk_ref[t:2 * t, :] = f1[:, B_W:].astype(BF16)

    o_ref[...] = jnp.dot(dft_ref[...], stk_ref[...], preferred_element_type=F32).astype(o_ref.dtype)


def _fourier(p, w1, dft2, prev, *, nbatch, t, row_blk0):
    fcol = (A_Q_W + 2 * A_KV_W) // B_W
    tq = _tile(512, t)
    nq = t // tq
    in_specs = [pl.BlockSpec((t, B_W), lambda b, q: (row_blk0 + b, fcol)),
                pl.BlockSpec((B_W, 2 * B_W), lambda b, q: (0, 0)),
                pl.BlockSpec((tq, 2 * t), lambda b, q: (q, 0))]
    args = [p, w1, dft2]
    aliases = {}
    if prev is not None:
        in_specs.append(pl.BlockSpec(memory_space=pl.ANY))
        args.append(prev)
        aliases = {3: 0}
    return pl.pallas_call(
        _fourier_kernel,
        grid=(nbatch, nq),
        in_specs=in_specs,
        out_specs=pl.BlockSpec((tq, B_W), lambda b, q: ((row_blk0 + b) * nq + q, 0)),
        out_shape=jax.ShapeDtypeStruct((p.shape[0], B_W), BF16),
        scratch_shapes=[pltpu.VMEM((2 * t, B_W), BF16)],
        input_output_aliases=aliases,
        compiler_params=_params("arbitrary", "arbitrary"),
        name="fourier",
    )(*args)


def _fourier_consts(t):
    cd, sd = _dft_tables(B_GROUP_DIM)
    eye = jnp.eye(B_GROUPS, dtype=F32)
    w1 = jnp.concatenate([jnp.kron(eye, cd), jnp.kron(eye, sd)], axis=1).astype(BF16)
    ct, st = _dft_tables(t)
    dft2 = jnp.concatenate([ct, -st], axis=1).astype(BF16)
    return w1, dft2


def _conv_prep_kernel(p_ref, w_ref, *rest, tc):
    o_ref, pad_ref = rest[-2:]
    j = pl.program_id(1)
    t = p_ref.shape[0]
    half = C_CONV // 2
    pad_ref[0:8, :] = jnp.zeros((8, tc), F32)
    pad_ref[t + 8:t + 16, :] = jnp.zeros((8, tc), F32)
    pad_ref[8:t + 8, :] = p_ref[...].astype(F32)
    w = w_ref[...]
    is_q = j < (C_W // tc)
    is_qk = j < (2 * C_W // tc)
    post = jnp.where(is_q, C_DK ** -0.5, 1.0)
    rc = _tile(256, t)
    for r0 in range(0, t, rc):
        acc = w[0:1, :] * pad_ref[r0 + 8 - half:r0 + 8 - half + rc, :]
        for jj in range(1, C_CONV):
            acc = acc + w[jj:jj + 1, :] * pad_ref[r0 + 8 - half + jj:r0 + 8 - half + jj + rc, :]
        y = _silu(acc)
        for hh in range(tc // C_DK):
            sl = slice(hh * C_DK, (hh + 1) * C_DK)
            yh = y[:, sl]
            nrm = lax.rsqrt(jnp.sum(yh * yh, axis=-1, keepdims=True) + EPS) * post
            o_ref[r0:r0 + rc, sl] = (yh * jnp.where(is_qk, nrm, 1.0)).astype(o_ref.dtype)


def _conv_prep(p, conv_w, prev, *, nseq, t, row_blk0):
    tc = 512
    ncol = 3 * C_W // tc
    in_specs = [pl.BlockSpec((t, tc), lambda b, j: (row_blk0 + b, j)),
                pl.BlockSpec((C_CONV, tc), lambda b, j: (0, j))]
    args = [p, conv_w]
    aliases = {}
    if prev is not None:
        in_specs.append(pl.BlockSpec(memory_space=pl.ANY))
        args.append(prev)
        aliases = {2: 0}
    return pl.pallas_call(
        functools.partial(_conv_prep_kernel, tc=tc),
        grid=(nseq, ncol),
        in_specs=in_specs,
        out_specs=pl.BlockSpec((t, tc), lambda b, j: (row_blk0 + b, j)),
        out_shape=jax.ShapeDtypeStruct((p.shape[0], 3 * C_W), BF16),
        scratch_shapes=[pltpu.VMEM((t + 16, tc), F32)],
        input_output_aliases=aliases,
        compiler_params=_params("arbitrary", "arbitrary"),
        name="conv_prep",
    )(*args)


HEADS_PER_PACK = 4
PACK_W = HEADS_PER_PACK * C_CHUNK
PACK_C = HEADS_PER_PACK * C_DK


def _split3(a):
    hi = a.astype(BF16)
    r1 = a - hi.astype(F32)
    mid = r1.astype(BF16)
    lo = (r1 - mid.astype(F32)).astype(BF16)
    return hi, mid, lo


def _dot3(m, a):
    hi, mid, lo = _split3(a)
    return (jnp.dot(m, hi, preferred_element_type=F32) + jnp.dot(m, mid, preferred_element_type=F32)
            + jnp.dot(m, lo, preferred_element_type=F32))


def _block_diag(x, mask):
    xb = x.astype(BF16)
    return jnp.where(mask, jnp.concatenate([xb] * HEADS_PER_PACK, axis=0), jnp.zeros((), BF16))


def _delta_kernel(q_ref, k_ref, v_ref, gt_ref, alog_ref, dtb_ref, o_ref, s_ref, *, reverse, d_idx):
    step = pl.program_id(1)
    cc = C_CHUNK
    npack = C_HEADS // HEADS_PER_PACK

    @pl.when(step == 0)
    def _():
        s_ref[...] = jnp.zeros(s_ref.shape, F32)

    gt = gt_ref[...]
    x = gt + dtb_ref[...]
    softplus = jnp.maximum(x, 0.0) + jnp.log(1.0 + jnp.exp(-jnp.abs(x)))
    g_all = -jnp.exp(alog_ref[...]) * softplus
    beta_all = _sigmoid(gt)

    ri = lax.broadcasted_iota(jnp.int32, (cc, cc), 0)
    ci = lax.broadcasted_iota(jnp.int32, (cc, cc), 1)
    rw = lax.broadcasted_iota(jnp.int32, (cc, PACK_W), 0)
    lw = lax.broadcasted_iota(jnp.int32, (cc, PACK_W), 1)
    cw = lw % cc
    hw = lw // cc
    if reverse:
        incl_sq, incl, strict, last = ci >= ri, cw >= rw, cw > rw, 0
    else:
        incl_sq, incl, strict, last = ci <= ri, cw <= rw, cw < rw, cc - 1
    eye_w = cw == rw
    tri = jnp.where(incl_sq, 1.0, 0.0).astype(BF16)
    ones = jnp.ones((cc, cc), BF16)
    bd_mask_w = (lax.broadcasted_iota(jnp.int32, (PACK_W, PACK_W), 0) // cc
                 == lax.broadcasted_iota(jnp.int32, (PACK_W, PACK_W), 1) // cc)
    bd_mask_c = (lax.broadcasted_iota(jnp.int32, (PACK_W, PACK_C), 0) // cc
                 == lax.broadcasted_iota(jnp.int32, (PACK_W, PACK_C), 1) // C_DK)

    gc_all = _dot3(tri, g_all)
    g_last = gc_all[last:last + 1, :]
    egc_all = jnp.exp(gc_all)
    edl_all = jnp.exp(g_last - gc_all)
    eg_last = jnp.exp(g_last)

    def lanes(arr, cols, width):
        return jnp.concatenate([jnp.broadcast_to(arr[:, c:c + 1], (cc, width)) for c in cols], axis=1)

    packs = []
    for p in range(npack):
        cols = [d_idx * C_HEADS + p * HEADS_PER_PACK + j for j in range(HEADS_PER_PACK)]
        sl = slice(p * PACK_C, (p + 1) * PACK_C)
        qn, kn, vn = q_ref[:, sl], k_ref[:, sl], v_ref[:, sl]
        kf = kn.astype(F32)
        beta_b = lanes(beta_all, [c + 2 * C_HEADS for c in cols], C_DK)
        egc_b = lanes(egc_all, cols, C_DK)
        kb = kf * beta_b
        vb = vn.astype(F32) * beta_b
        kbg = kb * egc_b
        qg = qn.astype(F32) * egc_b
        kdec = kf * lanes(edl_all, cols, C_DK)
        kkqk = _bdot_nt(jnp.concatenate([kb.astype(BF16), qn], axis=0), _block_diag(kn, bd_mask_c))
        gc_w = jnp.broadcast_to(gc_all[:, cols[-1]:cols[-1] + 1], (cc, PACK_W))
        for j in range(HEADS_PER_PACK - 1):
            gc_w = jnp.where(hw == j, gc_all[:, cols[j]:cols[j] + 1], gc_w)
        gr_w = _dot3(ones, jnp.where(eye_w, gc_w, 0.0))
        decay = jnp.where(incl, jnp.exp(jnp.where(incl, gc_w - gr_w, 0.0)), 0.0)
        nmat = jnp.where(strict, kkqk[:cc] * decay, 0.0)
        a_qk = jnp.where(incl, kkqk[cc:] * decay, 0.0)
        packs.append(dict(cols=cols, sl=sl, vb=vb, kbg=kbg, qg=qg, kdec=kdec, a_qk=a_qk,
                          xm=-nmat, tm=jnp.where(eye_w, 1.0, 0.0) - nmat))

    for pk in packs:
        pk["bd"] = _block_diag(pk["xm"], bd_mask_w)
    for _ in range(5):
        for pk in packs:
            pk["xm"] = jnp.dot(pk["xm"].astype(BF16), pk["bd"], preferred_element_type=F32)
        for pk in packs:
            pk["bd"] = _block_diag(pk["xm"], bd_mask_w)
        for pk in packs:
            pk["tm"] = pk["tm"] + jnp.dot(pk["tm"].astype(BF16), pk["bd"], preferred_element_type=F32)

    for pk in packs:
        tmb = pk["tm"].astype(BF16)
        pk["u"] = jnp.dot(tmb, _block_diag(pk["vb"], bd_mask_c), preferred_element_type=F32)
        pk["w"] = jnp.dot(tmb, _block_diag(pk["kbg"], bd_mask_c), preferred_element_type=F32)

    for p, pk in enumerate(packs):
        v_new, q_s = [], []
        for j in range(HEADS_PER_PACK):
            hs = slice(j * C_DK, (j + 1) * C_DK)
            ws_qs = _bdot(jnp.concatenate([pk["w"][:, hs], pk["qg"][:, hs]], axis=0), s_ref[p * HEADS_PER_PACK + j])
            v_new.append(pk["u"][:, hs] - ws_qs[:cc])
            q_s.append(ws_qs[cc:])
        pk["v_new"] = v_new
        pk["q_s"] = jnp.concatenate(q_s, axis=1)

    for p, pk in enumerate(packs):
        v_new_c = jnp.concatenate(pk["v_new"], axis=1)
        o = pk["q_s"] + jnp.dot(pk["a_qk"].astype(BF16), _block_diag(v_new_c, bd_mask_c), preferred_element_type=F32)
        o_ref[:, pk["sl"]] = o.astype(o_ref.dtype)
        for j in range(HEADS_PER_PACK):
            hs = slice(j * C_DK, (j + 1) * C_DK)
            h = p * HEADS_PER_PACK + j
            c = pk["cols"][j]
            s_ref[h] = s_ref[h] * eg_last[:, c:c + 1] + _bdot_tn(pk["kdec"][:, hs], pk["v_new"][j])


def _delta_scan(qkv, gates, a_log, dt_bias, *, reverse, nbatch, seq, n_ctx):
    tt = qkv.shape[0]
    cc = C_CHUNK
    lc, nc = n_ctx // cc, seq // cc
    n_lat_blk = nbatch * nc
    d_idx = 1 if reverse else 0

    def row_blk(b, s):
        if reverse:
            return jnp.where(s < lc, n_lat_blk + b * lc + (lc - 1 - s), b * nc + (nc - 1 - (s - lc)))
        return jnp.where(s < lc, n_lat_blk + b * lc + s, b * nc + (s - lc))

    def col_spec(base):
        return pl.BlockSpec((cc, C_W), lambda b, s: (row_blk(b, s), base))

    pad = jnp.zeros((1, GATE_W - 2 * C_HEADS), F32)
    alog_row = jnp.concatenate([a_log.reshape(1, 2 * C_HEADS), pad], axis=1)
    dtb_row = jnp.concatenate([dt_bias.reshape(1, 2 * C_HEADS), pad], axis=1)
    return pl.pallas_call(
        functools.partial(_delta_kernel, reverse=reverse, d_idx=d_idx),
        grid=(nbatch, lc + nc),
        in_specs=[col_spec(0), col_spec(1), col_spec(2),
                  pl.BlockSpec((cc, GATE_W), lambda b, s: (row_blk(b, s), 0)),
                  pl.BlockSpec((1, GATE_W), lambda b, s: (0, 0)),
                  pl.BlockSpec((1, GATE_W), lambda b, s: (0, 0))],
        out_specs=pl.BlockSpec((cc, C_W), lambda b, s: (row_blk(b, s), 0)),
        out_shape=jax.ShapeDtypeStruct((tt, C_W), BF16),
        scratch_shapes=[pltpu.VMEM((C_HEADS, C_DK, C_DK), F32)],
        compiler_params=_params("arbitrary", "arbitrary"),
        name="delta_bwd" if reverse else "delta_fwd",
    )(qkv, qkv, qkv, gates, alog_row, dtb_row)


def _gated_norm_kernel(of_ref, ob_ref, z_ref, nw_ref, o_ref):
    nw = nw_ref[...]
    for h in range(C_HEADS):
        sl = slice(h * C_DK, (h + 1) * C_DK)
        o = of_ref[:, sl].astype(F32) + ob_ref[:, sl].astype(F32)
        on = o * lax.rsqrt(jnp.mean(o * o, axis=-1, keepdims=True) + EPS) * nw
        o_ref[:, sl] = (on * _silu(z_ref[:, sl].astype(F32))).astype(o_ref.dtype)


def _gated_norm(o_f, o_b, p, norm_w):
    tt = o_f.shape[0]
    tm = _tile(512, tt)
    zcol = 3 * C_W // C_W
    return pl.pallas_call(
        _gated_norm_kernel,
        grid=(tt // tm,),
        in_specs=[pl.BlockSpec((tm, C_W), lambda i: (i, 0)),
                  pl.BlockSpec((tm, C_W), lambda i: (i, 0)),
                  pl.BlockSpec((tm, C_W), lambda i: (i, zcol)),
                  pl.BlockSpec((1, C_DK), lambda i: (0, 0))],
        out_specs=pl.BlockSpec((tm, C_W), lambda i: (i, 0)),
        out_shape=jax.ShapeDtypeStruct((tt, C_W), BF16),
        compiler_params=_params("arbitrary"),
        name="gated_norm",
    )(o_f, o_b, p, norm_w.reshape(1, C_DK))


def _max_first(vals):
    bv = vals[0]
    bi = jnp.zeros(vals[0].shape, jnp.int32)
    for idx in range(1, len(vals)):
        upd = vals[idx] > bv
        bi = jnp.where(upd, idx, bi)
        bv = jnp.where(upd, vals[idx], bv)
    return bv, bi


def _pick(vals, idx):
    out = vals[0]
    for n in range(1, len(vals)):
        out = jnp.where(idx == n, vals[n], out)
    return out


def _router_kernel(x_ref, mod_ref, nw_ref, rwt_ref, rb_ref, v_ref, route_ref):
    v = _norm_mod(x_ref[...], nw_ref[...], mod_ref[...], 3, 4)
    v_ref[...] = v
    vh, vl = _split(v)
    wh, wl = _split(rwt_ref[...])
    logits = _bdot_nt(wh, vh) + _bdot_nt(wh, vl) + _bdot_nt(wl, vh)
    scores = _sigmoid(logits)
    sel = scores + rb_ref[...]
    sel_r = [sel[e:e + 1, :] for e in range(N_EXPERTS)]
    sc_r = [scores[e:e + 1, :] for e in range(N_EXPERTS)]
    epg = EXPERTS_PER_GROUP
    gscore = []
    for g in range(N_GROUPS):
        a, b, c, d = sel_r[g * epg:(g + 1) * epg]
        m_ab, n_ab = jnp.maximum(a, b), jnp.minimum(a, b)
        m_cd, n_cd = jnp.maximum(c, d), jnp.minimum(c, d)
        top1 = jnp.maximum(m_ab, m_cd)
        top2 = jnp.maximum(jnp.minimum(m_ab, m_cd), jnp.maximum(n_ab, n_cd))
        gscore.append(top1 + top2)
    _, best = _max_first(gscore)
    in_sel = [_pick([sel_r[g * epg + k] for g in range(N_GROUPS)], best) for k in range(epg)]
    in_sc = [_pick([sc_r[g * epg + k] for g in range(N_GROUPS)], best) for k in range(epg)]
    _, i1 = _max_first(in_sel)
    _, i2 = _max_first([jnp.where(i1 == k, -jnp.inf, in_sel[k]) for k in range(epg)])
    s1 = _pick(in_sc, i1)
    s2 = _pick(in_sc, i2)
    tot = s1 + s2
    rows = [(best * epg + i1).astype(F32), (best * epg + i2).astype(F32), s1 / tot, s2 / tot]
    rows += [jnp.zeros_like(s1)] * 4
    route_ref[...] = jnp.concatenate(rows, axis=0)


def _router(xs, mod3, nw, router_w, router_b, *, n_lat, seq):
    tt, d = xs.shape
    nb = mod3.shape[0] - 1
    tm = _tile(256, seq, tt - n_lat, n_lat)
    per_seq = seq // tm
    nt = tt // tm
    return pl.pallas_call(
        _router_kernel,
        grid=(nt,),
        in_specs=[pl.BlockSpec((tm, d), lambda i: (i, 0)),
                  pl.BlockSpec((None, 6, d), lambda i: (jnp.minimum(i // per_seq, nb), 0, 0)),
                  pl.BlockSpec((1, d), lambda i: (0, 0)),
                  pl.BlockSpec((N_EXPERTS, d), lambda i: (0, 0)),
                  pl.BlockSpec((N_EXPERTS, 1), lambda i: (0, 0))],
        out_specs=[pl.BlockSpec((tm, d), lambda i: (i, 0)),
                   pl.BlockSpec((None, 8, tm), lambda i: (i, 0, 0))],
        out_shape=[jax.ShapeDtypeStruct((tt, d), F32),
                   jax.ShapeDtypeStruct((nt, 8, tm), F32)],
        compiler_params=_params("arbitrary"),
        name="router",
    )(xs, mod3, nw, router_w.T, router_b.reshape(N_EXPERTS, 1))


def _load_tile_indices(dest_hbm, idx_smem, sem, n):
    cp = pltpu.make_async_copy(dest_hbm.at[pl.ds(pl.program_id(0) * n, n)], idx_smem, sem)
    cp.start()
    cp.wait()


def _dispatch_kernel(dest_hbm, v_ref, init_hbm, xs_hbm, idx_smem, sem_idx, sem, *, tm):
    del init_hbm
    _load_tile_indices(dest_hbm, idx_smem, sem_idx, 2 * tm)

    def row_copy(t, d):
        return pltpu.make_async_copy(v_ref.at[pl.ds(t, 1)], xs_hbm.at[pl.ds(d, 1)], sem)

    def issue(t, c):
        row_copy(t, idx_smem[t]).start()
        row_copy(t, idx_smem[tm + t]).start()
        return c

    lax.fori_loop(0, tm, issue, 0, unroll=8)
    for _ in range(2):
        pltpu.make_async_copy(v_ref, xs_hbm.at[pl.ds(0, tm)], sem).wait()


def _dispatch(dest_tiles, v, xs_init, *, tm):
    tt, d = v.shape
    return pl.pallas_call(
        functools.partial(_dispatch_kernel, tm=tm),
        grid=(tt // tm,),
        in_specs=[pl.BlockSpec(memory_space=pl.ANY),
                  pl.BlockSpec((tm, d), lambda i: (i, 0)),
                  pl.BlockSpec(memory_space=pl.ANY)],
        out_specs=pl.BlockSpec(memory_space=pl.ANY),
        out_shape=jax.ShapeDtypeStruct(xs_init.shape, F32),
        scratch_shapes=[pltpu.SMEM((2 * tm,), jnp.int32), pltpu.SemaphoreType.DMA(()), pltpu.SemaphoreType.DMA(())],
        input_output_aliases={2: 0},
        compiler_params=_params("arbitrary"),
        name="moe_dispatch",
    )(dest_tiles, v, xs_init)


def _expert_kernel(te_ref, tv_ref, x_ref, wg_ref, wu_ref, wd_ref, o_ref):
    i = pl.program_id(0)

    @pl.when(tv_ref[i] > 0)
    def _():
        x = x_ref[...].astype(BF16)
        hg = jnp.dot(x, wg_ref[...], preferred_element_type=F32)
        hu = jnp.dot(x, wu_ref[...], preferred_element_type=F32)
        act = (_silu(hg) * hu).astype(BF16)
        o_ref[...] = jnp.dot(act, wd_ref[...], preferred_element_type=F32)

    @pl.when(tv_ref[i] == 0)
    def _():
        o_ref[...] = jnp.zeros(o_ref.shape, F32)


def _experts(tile_expert, tile_valid, xs, wg, wu, wd, *, tm):
    rows, d = xs.shape
    dff = wg.shape[2]
    grid_spec = pltpu.PrefetchScalarGridSpec(
        num_scalar_prefetch=2,
        grid=(rows // tm,),
        in_specs=[pl.BlockSpec((tm, d), lambda i, te, tv: (i, 0)),
                  pl.BlockSpec((None, d, dff), lambda i, te, tv: (te[i], 0, 0)),
                  pl.BlockSpec((None, d, dff), lambda i, te, tv: (te[i], 0, 0)),
                  pl.BlockSpec((None, dff, d), lambda i, te, tv: (te[i], 0, 0))],
        out_specs=pl.BlockSpec((tm, d), lambda i, te, tv: (i, 0)),
    )
    return pl.pallas_call(
        _expert_kernel,
        grid_spec=grid_spec,
        out_shape=jax.ShapeDtypeStruct((rows, d), F32),
        compiler_params=_params("arbitrary"),
        name="moe_experts",
    )(tile_expert, tile_valid, xs, wg, wu, wd)


def _combine_kernel(dest_hbm, ys_hbm, x_ref, w_ref, mod_ref, o_ref, idx_smem, ya_ref, yb_ref, sem_idx, sem, *, tm):
    _load_tile_indices(dest_hbm, idx_smem, sem_idx, 2 * tm)

    def row_copy(d, dst_ref, t):
        return pltpu.make_async_copy(ys_hbm.at[pl.ds(d, 1)], dst_ref.at[pl.ds(t, 1)], sem)

    def issue(t, c):
        row_copy(idx_smem[t], ya_ref, t).start()
        row_copy(idx_smem[tm + t], yb_ref, t).start()
        return c

    lax.fori_loop(0, tm, issue, 0, unroll=8)
    pltpu.make_async_copy(ys_hbm.at[pl.ds(0, tm)], ya_ref, sem).wait()
    pltpu.make_async_copy(ys_hbm.at[pl.ds(0, tm)], yb_ref, sem).wait()
    w = w_ref[...]
    ff = w[:, 0:1] * ya_ref[...] + w[:, 1:2] * yb_ref[...]
    o_ref[...] = x_ref[...] + mod_ref[5:6, :] * ff


def _combine(dest_tiles, ys, xs, w12, mod3, *, tm, n_lat, seq):
    tt, d = xs.shape
    nb = mod3.shape[0] - 1
    per_seq = seq // tm
    return pl.pallas_call(
        functools.partial(_combine_kernel, tm=tm),
        grid=(tt // tm,),
        in_specs=[pl.BlockSpec(memory_space=pl.ANY),
                  pl.BlockSpec(memory_space=pl.ANY),
                  pl.BlockSpec((tm, d), lambda i: (i, 0)),
                  pl.BlockSpec((tm, 2), lambda i: (i, 0)),
                  pl.BlockSpec((None, 6, d), lambda i: (jnp.minimum(i // per_seq, nb), 0, 0))],
        out_specs=pl.BlockSpec((tm, d), lambda i: (i, 0)),
        out_shape=jax.ShapeDtypeStruct((tt, d), F32),
        scratch_shapes=[pltpu.SMEM((2 * tm,), jnp.int32), pltpu.VMEM((tm, d), F32), pltpu.VMEM((tm, d), F32),
                        pltpu.SemaphoreType.DMA(()), pltpu.SemaphoreType.DMA(())],
        compiler_params=_params("arbitrary"),
        name="moe_combine",
    )(dest_tiles, ys, xs, w12, mod3)


def _moe_sorted_rows(tt):
    tme = _tile(256, tt)
    return tme, (2 * tt) // tme + N_EXPERTS


def _moe(xs, xs_sorted_init, mod3, nw, router_w, router_b, wg, wu, wd, *, n_lat, seq):
    tt, d = xs.shape
    v, route = _router(xs, mod3, nw, router_w, router_b, n_lat=n_lat, seq=seq)
    route = route.transpose(1, 0, 2).reshape(8, tt)
    e12 = route[0:2].astype(jnp.int32)
    w12 = route[2:4].T

    tme, n_tiles = _moe_sorted_rows(tt)
    flat_e = e12.reshape(-1)
    onehot = (flat_e[:, None] == jnp.arange(N_EXPERTS, dtype=jnp.int32)[None, :]).astype(jnp.int32)
    csum = jnp.cumsum(onehot, axis=0)
    rank = jnp.sum(onehot * csum, axis=1) - 1
    counts = csum[-1]
    padded = ((counts + tme - 1) // tme) * tme
    ends = jnp.cumsum(padded)
    starts = ends - padded
    dest = (jnp.sum(onehot * starts[None, :], axis=1) + rank).reshape(2, tt)
    tile_row0 = jnp.arange(n_tiles, dtype=jnp.int32) * tme
    tile_expert = jnp.minimum(jnp.sum((tile_row0[:, None] >= ends[None, :]).astype(jnp.int32), axis=1), N_EXPERTS - 1)
    tile_valid = (tile_row0 < ends[-1]).astype(jnp.int32)

    tmd = _tile(512, seq, tt - n_lat, n_lat)
    dest_tiles = dest.reshape(2, tt // tmd, tmd).transpose(1, 0, 2).reshape(-1)
    xs_sorted = _dispatch(dest_tiles, v, xs_sorted_init, tm=tmd)
    ys = _experts(tile_expert, tile_valid, xs_sorted, wg, wu, wd, tm=tme)
    return _combine(dest_tiles, ys, xs, w12, mod3, tm=tmd, n_lat=n_lat, seq=seq), xs_sorted


def _final_norm_kernel(x_ref, nw_ref, o_ref):
    x = x_ref[...]
    o_ref[...] = x * lax.rsqrt(jnp.mean(x * x, axis=-1, keepdims=True) + EPS) * nw_ref[...]


def _final_norm(xs, nw, *, n_lat):
    d = xs.shape[1]
    tm = _tile(512, n_lat)
    return pl.pallas_call(
        _final_norm_kernel,
        grid=(n_lat // tm,),
        in_specs=[pl.BlockSpec((tm, d), lambda i: (i, 0)), pl.BlockSpec((1, d), lambda i: (0, 0))],
        out_specs=pl.BlockSpec((tm, d), lambda i: (i, 0)),
        out_shape=jax.ShapeDtypeStruct((n_lat, d), F32),
        compiler_params=_params("arbitrary"),
        name="final_norm",
    )(xs, nw.reshape(1, d))


def _rope_tables(seq):
    rows = seq // GRID_W
    r = jnp.repeat(jnp.arange(rows, dtype=F32), GRID_W)
    col = jnp.tile(jnp.arange(GRID_W, dtype=F32), rows)
    n_freq = HEAD_DIM // 4
    inv = ROPE_BASE ** (-jnp.arange(n_freq, dtype=F32) / n_freq)
    ar, ac = r[:, None] * inv, col[:, None] * inv
    cos_t = jnp.concatenate([jnp.cos(ar), jnp.cos(ar), jnp.cos(ac), jnp.cos(ac)], axis=1)
    sin_t = jnp.concatenate([-jnp.sin(ar), jnp.sin(ar), -jnp.sin(ac), jnp.sin(ac)], axis=1)
    return cos_t, sin_t


def _even_layer(xs, mod3, nw, w_in, w_out, sink, rope, fconst_lat, fconst_ctx, *, nbatch, seq, n_ctx):
    n_lat = nbatch * seq
    cos_t, sin_t = rope
    p = _norm_mm(xs, mod3, nw, w_in.astype(BF16), cos_t, sin_t, n_lat=n_lat, seq=seq, rope_cols=A_Q_W + A_KV_W)
    a = _win_attn(p, sink, nbatch=nbatch, seq=seq, n_ctx=n_ctx)
    a = _ctx_attn(p, sink, a, nbatch=nbatch, n_lat=n_lat, n_ctx=n_ctx)
    f = _fourier(p, *fconst_lat, None, nbatch=nbatch, t=seq, row_blk0=0)
    f = _fourier(p, *fconst_ctx, f, nbatch=nbatch, t=n_ctx, row_blk0=n_lat // n_ctx)
    wo = w_out.astype(BF16)
    return _res_mm([a, f], [wo[:A_Q_W], wo[A_Q_W:]], xs, mod3, gate_row=2, n_lat=n_lat, seq=seq)


def _odd_layer(xs, mod3, nw, w_in, conv_w, a_log, dt_bias, norm_w, w_out, rope, *, nbatch, seq, n_ctx):
    n_lat = nbatch * seq
    d = xs.shape[1]
    cos_t, sin_t = rope
    n_main = 4 * C_W
    w_main = w_in[:, :n_main].astype(BF16)
    w_gate = jnp.concatenate([w_in[:, n_main:], jnp.zeros((d, GATE_W - 4 * C_HEADS), F32)], axis=1).astype(BF16)
    p = _norm_mm(xs, mod3, nw, w_main, cos_t, sin_t, n_lat=n_lat, seq=seq, rope_cols=0)
    gates = _norm_mm(xs, mod3, nw, w_gate, cos_t, sin_t, n_lat=n_lat, seq=seq, rope_cols=0, out_dtype=F32)
    qkv = _conv_prep(p, conv_w, None, nseq=nbatch, t=seq, row_blk0=0)
    qkv = _conv_prep(p, conv_w, qkv, nseq=nbatch, t=n_ctx, row_blk0=n_lat // n_ctx)
    o_f = _delta_scan(qkv, gates, a_log, dt_bias, reverse=False, nbatch=nbatch, seq=seq, n_ctx=n_ctx)
    o_b = _delta_scan(qkv, gates, a_log, dt_bias, reverse=True, nbatch=nbatch, seq=seq, n_ctx=n_ctx)
    on = _gated_norm(o_f, o_b, p, norm_w)
    return _res_mm([on], [w_out.astype(BF16)], xs, mod3, gate_row=2, n_lat=n_lat, seq=seq)


def kernel(x, c, ctx, c_ctx, adaln_w, adaln_b, norm_mix_w, norm_ffn_w, attn_in_w, attn_out_w, attn_sink, dn_in_w, dn_conv_w, dn_a_log, dn_dt_bias, dn_norm_w, dn_out_w, router_w, router_b, exp_gate_w, exp_up_w, exp_down_w, final_norm_w):
    nbatch, seq, d = x.shape
    n_ctx = ctx.shape[1]
    depth = adaln_w.shape[0]
    n_lat = nbatch * seq
    assert seq % BLOCK == 0 and seq % GRID_W == 0 and n_ctx % C_CHUNK == 0 and n_lat % n_ctx == 0

    xs = jnp.concatenate([x.reshape(n_lat, d), ctx.reshape(nbatch * n_ctx, d)], axis=0)
    mod_rows = 8 * ((nbatch + 1 + 7) // 8)
    c_all = jnp.concatenate([c, c_ctx[None, :], jnp.zeros((mod_rows - nbatch - 1, d), F32)], axis=0)
    mod_all = _adaln(c_all, adaln_w, adaln_b)[:, :nbatch + 1].reshape(depth, nbatch + 1, 6, d)

    rope = _rope_tables(seq)
    fconst_lat = _fourier_consts(seq)
    fconst_ctx = _fourier_consts(n_ctx)
    tme, n_tiles = _moe_sorted_rows(xs.shape[0])
    xs_sorted = jnp.zeros((n_tiles * tme, d), F32)

    for layer in range(depth):
        mod3 = mod_all[layer]
        i = layer // 2
        nw = norm_mix_w[layer].reshape(1, d)
        if layer % 2 == 0:
            xs = _even_layer(xs, mod3, nw, attn_in_w[i], attn_out_w[i], attn_sink[i], rope, fconst_lat, fconst_ctx,
                             nbatch=nbatch, seq=seq, n_ctx=n_ctx)
        else:
            xs = _odd_layer(xs, mod3, nw, dn_in_w[i], dn_conv_w[i], dn_a_log[i], dn_dt_bias[i], dn_norm_w[i],
                            dn_out_w[i], rope, nbatch=nbatch, seq=seq, n_ctx=n_ctx)
        xs, xs_sorted = _moe(xs, xs_sorted, mod3, norm_ffn_w[layer].reshape(1, d), router_w, router_b,
                             exp_gate_w[layer].astype(BF16), exp_up_w[layer].astype(BF16),
                             exp_down_w[layer].astype(BF16), n_lat=n_lat, seq=seq)
    return _final_norm(xs, final_norm_w, n_lat=n_lat).reshape(nbatch, seq, d)
```

```python
import functools
import math

import jax
import jax.numpy as jnp
from jax import lax
from jax.experimental import pallas as pl
from jax.experimental.pallas import tpu as pltpu

F32 = jnp.float32
BF16 = jnp.bfloat16
EPS = 1e-6

HEAD_DIM = 128
A_Q_HEADS = 12
A_KV_HEADS = 4
A_GROUP = A_Q_HEADS // A_KV_HEADS
BLOCK = 128
GRID_W = 64
ROPE_BASE = 10000.0
B_GROUPS = 4
B_GROUP_DIM = 128
C_HEADS = 16
C_DK = 128
C_CONV = 5
C_CHUNK = 64
N_EXPERTS = 16
N_GROUPS = 4
EXPERTS_PER_GROUP = N_EXPERTS // N_GROUPS

A_Q_W = A_Q_HEADS * HEAD_DIM
A_KV_W = A_KV_HEADS * HEAD_DIM
B_W = B_GROUPS * B_GROUP_DIM
C_W = C_HEADS * C_DK
GATE_W = 128

V7X_VMEM_LIMIT_BYTES = 56 * 1024 * 1024
NEG_BIG = -1e30


def _tile(pref, *dims, mult=8):
    t = min((pref,) + dims)
    t -= t % mult
    while t > mult and any(d % t for d in dims):
        t -= mult
    assert t >= mult and all(d % t == 0 for d in dims), (pref, dims)
    return t


def _params(*sem):
    return pltpu.CompilerParams(dimension_semantics=sem, vmem_limit_bytes=V7X_VMEM_LIMIT_BYTES)


def _bdot(a, b):
    return jnp.dot(a.astype(BF16), b.astype(BF16), preferred_element_type=F32)


def _bdot_nt(a, b):
    return lax.dot_general(a.astype(BF16), b.astype(BF16), (((1,), (1,)), ((), ())), preferred_element_type=F32)


def _bdot_tn(a, b):
    return lax.dot_general(a.astype(BF16), b.astype(BF16), (((0,), (0,)), ((), ())), preferred_element_type=F32)


def _split(a):
    hi = a.astype(BF16)
    lo = (a - hi.astype(F32)).astype(BF16)
    return hi, lo


def _sigmoid(x):
    return 1.0 / (1.0 + jnp.exp(-x))


def _silu(x):
    return x * _sigmoid(x)


def _adaln_kernel(c_ref, w_ref, b_ref, o_ref):
    sc = _silu(c_ref[...])
    sh, sl = _split(sc)
    wh, wl = _split(w_ref[...])
    acc = (jnp.dot(sh, wh, preferred_element_type=F32) + jnp.dot(sh, wl, preferred_element_type=F32)
           + jnp.dot(sl, wh, preferred_element_type=F32))
    o_ref[...] = acc + b_ref[...]


def _adaln(c_all, adaln_w, adaln_b):
    depth, d, n6 = adaln_w.shape
    rows = c_all.shape[0]
    tn = _tile(512, n6, mult=128)
    return pl.pallas_call(
        _adaln_kernel,
        grid=(depth, n6 // tn),
        in_specs=[pl.BlockSpec((rows, d), lambda l, j: (0, 0)),
                  pl.BlockSpec((None, d, tn), lambda l, j: (l, 0, j)),
                  pl.BlockSpec((None, 1, tn), lambda l, j: (l, 0, j))],
        out_specs=pl.BlockSpec((None, rows, tn), lambda l, j: (l, 0, j)),
        out_shape=jax.ShapeDtypeStruct((depth, rows, n6), F32),
        compiler_params=_params("arbitrary", "arbitrary"),
        name="adaln",
    )(c_all, adaln_w, adaln_b.reshape(depth, 1, n6))


def _norm_mod(x, nw, mod, shift_row, scale_row):
    ms = jnp.mean(x * x, axis=-1, keepdims=True)
    y = x * lax.rsqrt(ms + EPS) * nw
    return y * (1.0 + mod[scale_row:scale_row + 1, :]) + mod[shift_row:shift_row + 1, :]


def _rope_slab(a, cos, sin):
    lane = lax.broadcasted_iota(jnp.int32, a.shape, 1)
    first = (lane % 64) < 32
    partner = jnp.where(first, pltpu.roll(a, 96, 1), pltpu.roll(a, 32, 1))
    return a * cos + partner * sin


def _norm_mm_kernel(x_ref, mod_ref, nw_ref, w_ref, cos_ref, sin_ref, *rest, rope_tiles, n_lat_tiles, tn):
    if len(rest) == 4:
        w2_ref, o_ref, o2_ref, u_ref = rest
    else:
        (o_ref, u_ref), w2_ref, o2_ref = rest, None, None
    i = pl.program_id(0)
    j = pl.program_id(1)

    @pl.when(j == 0)
    def _():
        u_ref[...] = _norm_mod(x_ref[...], nw_ref[...], mod_ref[...], 0, 1).astype(BF16)
        if w2_ref is not None:
            o2_ref[...] = jnp.dot(u_ref[...], w2_ref[...], preferred_element_type=F32)

    acc = jnp.dot(u_ref[...], w_ref[...], preferred_element_type=F32)
    if rope_tiles == 0:
        o_ref[...] = acc.astype(o_ref.dtype)
    else:
        do_rope = jnp.logical_and(j < rope_tiles, i < n_lat_tiles)

        @pl.when(do_rope)
        def _():
            cos = cos_ref[...]
            sin = sin_ref[...]
            for s in range(tn // HEAD_DIM):
                sl = slice(s * HEAD_DIM, (s + 1) * HEAD_DIM)
                o_ref[:, sl] = _rope_slab(acc[:, sl], cos, sin).astype(o_ref.dtype)

        @pl.when(jnp.logical_not(do_rope))
        def _():
            o_ref[...] = acc.astype(o_ref.dtype)


def _norm_mm(xs, mod3, nw, w, cos_t, sin_t, *, n_lat, seq, rope_cols, w_side=None):
    tt, d = xs.shape
    nout = w.shape[1]
    nb = mod3.shape[0] - 1
    tm = _tile(512, seq, tt - n_lat, n_lat)
    tn = _tile(1024, nout, *((rope_cols,) if rope_cols else ()), mult=128)
    per_seq = seq // tm
    kern = functools.partial(_norm_mm_kernel, rope_tiles=rope_cols // tn, n_lat_tiles=n_lat // tm, tn=tn)
    in_specs = [pl.BlockSpec((tm, d), lambda i, j: (i, 0)),
                pl.BlockSpec((None, 6, d), lambda i, j: (jnp.minimum(i // per_seq, nb), 0, 0)),
                pl.BlockSpec((1, d), lambda i, j: (0, 0)),
                pl.BlockSpec((d, tn), lambda i, j: (0, j)),
                pl.BlockSpec((tm, HEAD_DIM), lambda i, j: (i % per_seq, 0)),
                pl.BlockSpec((tm, HEAD_DIM), lambda i, j: (i % per_seq, 0))]
    out_specs = [pl.BlockSpec((tm, tn), lambda i, j: (i, j))]
    out_shape = [jax.ShapeDtypeStruct((tt, nout), BF16)]
    args = [xs, mod3, nw, w, cos_t, sin_t]
    if w_side is not None:
        nside = w_side.shape[1]
        in_specs.append(pl.BlockSpec((d, nside), lambda i, j: (0, 0)))
        out_specs.append(pl.BlockSpec((tm, nside), lambda i, j: (i, 0)))
        out_shape.append(jax.ShapeDtypeStruct((tt, nside), F32))
        args.append(w_side)
    outs = pl.pallas_call(
        kern,
        grid=(tt // tm, nout // tn),
        in_specs=in_specs,
        out_specs=out_specs,
        out_shape=out_shape,
        scratch_shapes=[pltpu.VMEM((tm, d), BF16)],
        compiler_params=_params("arbitrary", "arbitrary"),
        name="norm_mm",
    )(*args)
    return outs if w_side is not None else outs[0]


def _res_mm_kernel(*refs, n_parts, gate_row):
    a_refs = refs[:n_parts]
    w_refs = refs[n_parts:2 * n_parts]
    x_ref, mod_ref, o_ref = refs[2 * n_parts:]
    acc = jnp.dot(a_refs[0][...], w_refs[0][...], preferred_element_type=F32)
    for a_ref, w_ref in zip(a_refs[1:], w_refs[1:]):
        acc = acc + jnp.dot(a_ref[...], w_ref[...], preferred_element_type=F32)
    o_ref[...] = x_ref[...] + mod_ref[gate_row:gate_row + 1, :] * acc


def _res_mm(parts, weights, xs, mod3, *, gate_row, n_lat, seq):
    tt, d = xs.shape
    nb = mod3.shape[0] - 1
    tm = _tile(512, seq, tt - n_lat, n_lat)
    tn = _tile(1024, d, mult=128)
    per_seq = seq // tm
    n_parts = len(parts)
    in_specs = ([pl.BlockSpec((tm, a.shape[1]), lambda i, j: (i, 0)) for a in parts]
                + [pl.BlockSpec((w.shape[0], tn), lambda i, j: (0, j)) for w in weights]
                + [pl.BlockSpec((tm, tn), lambda i, j: (i, j)),
                   pl.BlockSpec((None, 6, tn), lambda i, j: (jnp.minimum(i // per_seq, nb), 0, j))])
    return pl.pallas_call(
        functools.partial(_res_mm_kernel, n_parts=n_parts, gate_row=gate_row),
        grid=(tt // tm, d // tn),
        in_specs=in_specs,
        out_specs=pl.BlockSpec((tm, tn), lambda i, j: (i, j)),
        out_shape=jax.ShapeDtypeStruct((tt, d), F32),
        compiler_params=_params("arbitrary", "arbitrary"),
        name="res_mm",
    )(*parts, *weights, xs, mod3)


def _softmax_sink_pv(s, sk, v):
    m = jnp.maximum(jnp.max(s, axis=-1, keepdims=True), sk)
    p = jnp.exp(s - m)
    den = jnp.sum(p, axis=-1, keepdims=True) + jnp.exp(sk - m)
    return _bdot(p, v) / den


def _sink_column(sink_ref, h, rows_per_head):
    r = lax.broadcasted_iota(jnp.int32, (A_GROUP * rows_per_head, 1), 0) // rows_per_head
    sk = jnp.zeros((A_GROUP * rows_per_head, 1), F32)
    for g in range(A_GROUP):
        sk = jnp.where(r == g, sink_ref[h * A_GROUP + g], sk)
    return sk


def _stack_heads(q):
    return jnp.concatenate([q[:, g * HEAD_DIM:(g + 1) * HEAD_DIM] for g in range(A_GROUP)], axis=0)


def _unstack_heads(o, rows):
    return jnp.concatenate([o[g * rows:(g + 1) * rows, :] for g in range(A_GROUP)], axis=1)


def _win_attn_kernel(sink_ref, q_ref, km_ref, k0_ref, kp_ref, kc_ref, vm_ref, v0_ref, vp_ref, vc_ref, o_ref, *, nblk):
    i = pl.program_id(1)
    n_ctx = kc_ref.shape[0]
    nk = 3 * BLOCK + n_ctx
    r = lax.broadcasted_iota(jnp.int32, (A_GROUP * BLOCK, nk), 0) % BLOCK
    c = lax.broadcasted_iota(jnp.int32, (A_GROUP * BLOCK, nk), 1)
    band = jnp.logical_and(c >= r, c <= r + 2 * BLOCK)
    lo_ok = jnp.logical_or(c >= BLOCK, i > 0)
    hi_ok = jnp.logical_or(c < 2 * BLOCK, i < nblk - 1)
    valid = jnp.logical_or(c >= 3 * BLOCK, jnp.logical_and(band, jnp.logical_and(lo_ok, hi_ok)))
    for h in range(A_KV_HEADS):
        hs = slice(h * HEAD_DIM, (h + 1) * HEAD_DIM)
        qsl = slice(h * A_GROUP * HEAD_DIM, (h + 1) * A_GROUP * HEAD_DIM)
        qs = _stack_heads(q_ref[:, qsl])
        k = jnp.concatenate([km_ref[:, hs], k0_ref[:, hs], kp_ref[:, hs], kc_ref[:, hs]], axis=0)
        v = jnp.concatenate([vm_ref[:, hs], v0_ref[:, hs], vp_ref[:, hs], vc_ref[:, hs]], axis=0)
        s = jnp.where(valid, _bdot_nt(qs, k) * (HEAD_DIM ** -0.5), NEG_BIG)
        o = _softmax_sink_pv(s, _sink_column(sink_ref, h, BLOCK), v)
        o_ref[:, qsl] = _unstack_heads(o, BLOCK).astype(o_ref.dtype)


def _win_attn(p, sink, *, nbatch, seq, n_ctx):
    tt = p.shape[0]
    nblk = seq // BLOCK
    n_lat = nbatch * seq
    ctx_blk0 = n_lat // n_ctx
    kcol = A_Q_W // A_KV_W
    vcol = (A_Q_W + A_KV_W) // A_KV_W

    def kv_spec(col, off):
        return pl.BlockSpec((BLOCK, A_KV_W), lambda b, i, s: (b * nblk + jnp.clip(i + off, 0, nblk - 1), col))

    def ctx_spec(col):
        return pl.BlockSpec((n_ctx, A_KV_W), lambda b, i, s: (ctx_blk0 + b, col))

    grid_spec = pltpu.PrefetchScalarGridSpec(
        num_scalar_prefetch=1,
        grid=(nbatch, nblk),
        in_specs=[pl.BlockSpec((BLOCK, A_Q_W), lambda b, i, s: (b * nblk + i, 0)),
                  kv_spec(kcol, -1), kv_spec(kcol, 0), kv_spec(kcol, 1), ctx_spec(kcol),
                  kv_spec(vcol, -1), kv_spec(vcol, 0), kv_spec(vcol, 1), ctx_spec(vcol)],
        out_specs=pl.BlockSpec((BLOCK, A_Q_W), lambda b, i, s: (b * nblk + i, 0)),
    )
    return pl.pallas_call(
        functools.partial(_win_attn_kernel, nblk=nblk),
        grid_spec=grid_spec,
        out_shape=jax.ShapeDtypeStruct((tt, A_Q_W), BF16),
        compiler_params=_params("arbitrary", "arbitrary"),
        name="win_attn",
    )(sink, p, p, p, p, p, p, p, p, p)


def _ctx_attn_kernel(sink_ref, q_ref, k_ref, v_ref, prev_ref, o_ref):
    del prev_ref
    h = pl.program_id(1)
    n_ctx = q_ref.shape[0]
    s = _bdot_nt(_stack_heads(q_ref[...]), k_ref[...]) * (HEAD_DIM ** -0.5)
    o = _softmax_sink_pv(s, _sink_column(sink_ref, h, n_ctx), v_ref[...])
    o_ref[...] = _unstack_heads(o, n_ctx).astype(o_ref.dtype)


def _ctx_attn(p, sink, a_lat, *, nbatch, n_lat, n_ctx):
    ctx_blk0 = n_lat // n_ctx
    kcol = A_Q_W // HEAD_DIM
    vcol = (A_Q_W + A_KV_W) // HEAD_DIM
    grid_spec = pltpu.PrefetchScalarGridSpec(
        num_scalar_prefetch=1,
        grid=(nbatch, A_KV_HEADS),
        in_specs=[pl.BlockSpec((n_ctx, A_GROUP * HEAD_DIM), lambda b, h, s: (ctx_blk0 + b, h)),
                  pl.BlockSpec((n_ctx, HEAD_DIM), lambda b, h, s: (ctx_blk0 + b, kcol + h)),
                  pl.BlockSpec((n_ctx, HEAD_DIM), lambda b, h, s: (ctx_blk0 + b, vcol + h)),
                  pl.BlockSpec(memory_space=pl.ANY)],
        out_specs=pl.BlockSpec((n_ctx, A_GROUP * HEAD_DIM), lambda b, h, s: (ctx_blk0 + b, h)),
    )
    return pl.pallas_call(
        _ctx_attn_kernel,
        grid_spec=grid_spec,
        out_shape=jax.ShapeDtypeStruct(a_lat.shape, BF16),
        input_output_aliases={4: 0},
        compiler_params=_params("arbitrary", "arbitrary"),
        name="ctx_attn",
    )(sink, p, p, p, a_lat)


def _dft_tables(t):
    k = jnp.arange(t, dtype=jnp.int32)
    ang = ((k[:, None] * k[None, :]) % t).astype(F32) * (2.0 * math.pi / t)
    scale = t ** -0.5
    return jnp.cos(ang) * scale, jnp.sin(ang) * scale


def _fourier_kernel(f_ref, w1_ref, dft_ref, *rest):
    o_ref, stk_ref = rest[-2:]
    t = f_ref.shape[0]

    @pl.when(pl.program_id(1) == 0)
    def _():
        f1 = jnp.dot(f_ref[...], w1_ref[...], preferred_element_type=F32)
        stk_ref[0:t, :] = f1[:, :B_W].astype(BF16)
        stk_ref[t:2 * t, :] = f1[:, B_W:].astype(BF16)

    o_ref[...] = jnp.dot(dft_ref[...], stk_ref[...], preferred_element_type=F32).astype(o_ref.dtype)


def _fourier(p, w1, dft2, prev, *, nbatch, t, row_blk0):
    fcol = (A_Q_W + 2 * A_KV_W) // B_W
    tq = _tile(512, t)
    nq = t // tq
    in_specs = [pl.BlockSpec((t, B_W), lambda b, q: (row_blk0 + b, fcol)),
                pl.BlockSpec((B_W, 2 * B_W), lambda b, q: (0, 0)),
                pl.BlockSpec((tq, 2 * t), lambda b, q: (q, 0))]
    args = [p, w1, dft2]
    aliases = {}
    if prev is not None:
        in_specs.append(pl.BlockSpec(memory_space=pl.ANY))
        args.append(prev)
        aliases = {3: 0}
    return pl.pallas_call(
        _fourier_kernel,
        grid=(nbatch, nq),
        in_specs=in_specs,
        out_specs=pl.BlockSpec((tq, B_W), lambda b, q: ((row_blk0 + b) * nq + q, 0)),
        out_shape=jax.ShapeDtypeStruct((p.shape[0], B_W), BF16),
        scratch_shapes=[pltpu.VMEM((2 * t, B_W), BF16)],
        input_output_aliases=aliases,
        compiler_params=_params("arbitrary", "arbitrary"),
        name="fourier",
    )(*args)


def _fourier_consts(t):
    cd, sd = _dft_tables(B_GROUP_DIM)
    eye = jnp.eye(B_GROUPS, dtype=F32)
    w1 = jnp.concatenate([jnp.kron(eye, cd), jnp.kron(eye, sd)], axis=1).astype(BF16)
    ct, st = _dft_tables(t)
    dft2 = jnp.concatenate([ct, -st], axis=1).astype(BF16)
    return w1, dft2


def _conv_prep_kernel(p_ref, w_ref, *rest, tc):
    o_ref, pad_ref = rest[-2:]
    j = pl.program_id(1)
    t = p_ref.shape[0]
    half = C_CONV // 2
    pad_ref[0:8, :] = jnp.zeros((8, tc), F32)
    pad_ref[t + 8:t + 16, :] = jnp.zeros((8, tc), F32)
    pad_ref[8:t + 8, :] = p_ref[...].astype(F32)
    w = w_ref[...]
    is_q = j < (C_W // tc)
    is_qk = j < (2 * C_W // tc)
    post = jnp.where(is_q, C_DK ** -0.5, 1.0)
    rc = _tile(256, t)
    for r0 in range(0, t, rc):
        win = pad_ref[r0:r0 + rc + 16, :]
        acc = w[half:half + 1, :] * win[8:8 + rc, :]
        for jj in range(C_CONV):
            if jj != half:
                shifted = pltpu.roll(win, (half - jj) % (rc + 16), 0)
                acc = acc + w[jj:jj + 1, :] * shifted[8:8 + rc, :]
        y = _silu(acc)
        for hh in range(tc // C_DK):
            sl = slice(hh * C_DK, (hh + 1) * C_DK)
            yh = y[:, sl]
            nrm = lax.rsqrt(jnp.sum(yh * yh, axis=-1, keepdims=True) + EPS) * post
            o_ref[r0:r0 + rc, sl] = (yh * jnp.where(is_qk, nrm, 1.0)).astype(o_ref.dtype)


def _conv_prep(p, conv_w, prev, *, nseq, t, row_blk0):
    tc = 512
    ncol = 3 * C_W // tc
    in_specs = [pl.BlockSpec((t, tc), lambda b, j: (row_blk0 + b, j)),
                pl.BlockSpec((C_CONV, tc), lambda b, j: (0, j))]
    args = [p, conv_w]
    aliases = {}
    if prev is not None:
        in_specs.append(pl.BlockSpec(memory_space=pl.ANY))
        args.append(prev)
        aliases = {2: 0}
    return pl.pallas_call(
        functools.partial(_conv_prep_kernel, tc=tc),
        grid=(nseq, ncol),
        in_specs=in_specs,
        out_specs=pl.BlockSpec((t, tc), lambda b, j: (row_blk0 + b, j)),
        out_shape=jax.ShapeDtypeStruct((p.shape[0], 3 * C_W), BF16),
        scratch_shapes=[pltpu.VMEM((t + 16, tc), F32)],
        input_output_aliases=aliases,
        compiler_params=_params("arbitrary", "arbitrary"),
        name="conv_prep",
    )(*args)


HEADS_PER_PACK = 2
PACK_W = HEADS_PER_PACK * C_CHUNK
PACK_C = HEADS_PER_PACK * C_DK


def _split3(a):
    hi = a.astype(BF16)
    r1 = a - hi.astype(F32)
    mid = r1.astype(BF16)
    lo = (r1 - mid.astype(F32)).astype(BF16)
    return hi, mid, lo


def _dot3(m, a):
    hi, mid, lo = _split3(a)
    return (jnp.dot(m, hi, preferred_element_type=F32) + jnp.dot(m, mid, preferred_element_type=F32)
            + jnp.dot(m, lo, preferred_element_type=F32))


def _block_diag(x, mask):
    xb = x.astype(BF16)
    return jnp.where(mask, jnp.concatenate([xb] * HEADS_PER_PACK, axis=0), jnp.zeros((), BF16))


def _delta_kernel(*refs):
    ndir = 2
    in_refs = [refs[4 * s:4 * s + 4] for s in range(ndir)]
    alog_ref, dtb_ref = refs[4 * ndir:4 * ndir + 2]
    o_refs = refs[4 * ndir + 2:5 * ndir + 2]
    s_ref = refs[5 * ndir + 2]
    step = pl.program_id(1)
    cc = C_CHUNK
    npack = C_HEADS // HEADS_PER_PACK

    @pl.when(step == 0)
    def _():
        s_ref[...] = jnp.zeros(s_ref.shape, F32)

    ri = lax.broadcasted_iota(jnp.int32, (cc, cc), 0)
    ci = lax.broadcasted_iota(jnp.int32, (cc, cc), 1)
    rw = lax.broadcasted_iota(jnp.int32, (cc, PACK_W), 0)
    lw = lax.broadcasted_iota(jnp.int32, (cc, PACK_W), 1)
    cw = lw % cc
    hw = lw // cc
    dir_masks = [(ci <= ri, cw <= rw, cw < rw, cc - 1), (ci >= ri, cw >= rw, cw > rw, 0)]
    eye_w = cw == rw
    ones = jnp.ones((cc, cc), BF16)
    bd_mask_w = (lax.broadcasted_iota(jnp.int32, (PACK_W, PACK_W), 0) // cc
                 == lax.broadcasted_iota(jnp.int32, (PACK_W, PACK_W), 1) // cc)
    bd_mask_c = (lax.broadcasted_iota(jnp.int32, (PACK_W, PACK_C), 0) // cc
                 == lax.broadcasted_iota(jnp.int32, (PACK_W, PACK_C), 1) // C_DK)

    def lanes(arr, cols, width):
        return jnp.concatenate([jnp.broadcast_to(arr[:, c:c + 1], (cc, width)) for c in cols], axis=1)

    packs = []
    for si, p in [(si, p) for si in range(ndir) for p in range(npack)]:
        q_ref, k_ref, v_ref, gt_ref = in_refs[si]
        d_idx = si
        if p == 0:
            incl_sq, incl, strict, last = dir_masks[si]
            tri = jnp.where(incl_sq, 1.0, 0.0).astype(BF16)
            gt = gt_ref[...]
            x = gt + dtb_ref[...]
            softplus = jnp.maximum(x, 0.0) + jnp.log(1.0 + jnp.exp(-jnp.abs(x)))
            g_all = -jnp.exp(alog_ref[...]) * softplus
            beta_all = _sigmoid(gt)
            gc_all = _dot3(tri, g_all)
            g_last = gc_all[last:last + 1, :]
            egc_all = jnp.exp(gc_all)
            edl_all = jnp.exp(g_last - gc_all)
            eg_last = jnp.exp(g_last)
        cols = [d_idx * C_HEADS + p * HEADS_PER_PACK + j for j in range(HEADS_PER_PACK)]
        sl = slice(p * PACK_C, (p + 1) * PACK_C)
        qn, kn, vn = q_ref[:, sl], k_ref[:, sl], v_ref[:, sl]
        kf = kn.astype(F32)
        beta_b = lanes(beta_all, [c + 2 * C_HEADS for c in cols], C_DK)
        egc_b = lanes(egc_all, cols, C_DK)
        kb = kf * beta_b
        vb = vn.astype(F32) * beta_b
        kbg = kb * egc_b
        qg = qn.astype(F32) * egc_b
        kdec = kf * lanes(edl_all, cols, C_DK)
        kkqk = _bdot_nt(jnp.concatenate([kb.astype(BF16), qn], axis=0), _block_diag(kn, bd_mask_c))
        gc_w = jnp.broadcast_to(gc_all[:, cols[-1]:cols[-1] + 1], (cc, PACK_W))
        for j in range(HEADS_PER_PACK - 1):
            gc_w = jnp.where(hw == j, gc_all[:, cols[j]:cols[j] + 1], gc_w)
        gr_w = _dot3(ones, jnp.where(eye_w, gc_w, 0.0))
        decay = jnp.where(incl, jnp.exp(jnp.where(incl, gc_w - gr_w, 0.0)), 0.0)
        nmat = jnp.where(strict, kkqk[:cc] * decay, 0.0)
        a_qk = jnp.where(incl, kkqk[cc:] * decay, 0.0)
        packs.append(dict(cols=cols, sl=sl, vb=vb, kbg=kbg, qg=qg, kdec=kdec, a_qk=a_qk, eg_last=eg_last,
                          o_ref=o_refs[si], head0=si * C_HEADS + p * HEADS_PER_PACK,
                          xm=-nmat, tm=jnp.where(eye_w, 1.0, 0.0) - nmat))

    for pk in packs:
        pk["bd"] = _block_diag(pk["xm"], bd_mask_w)
    for _ in range(5):
        for pk in packs:
            pk["xm"] = jnp.dot(pk["xm"].astype(BF16), pk["bd"], preferred_element_type=F32)
        for pk in packs:
            pk["bd"] = _block_diag(pk["xm"], bd_mask_w)
        for pk in packs:
            pk["tm"] = pk["tm"] + jnp.dot(pk["tm"].astype(BF16), pk["bd"], preferred_element_type=F32)

    for pk in packs:
        tmb = pk["tm"].astype(BF16)
        pk["u"] = jnp.dot(tmb, _block_diag(pk["vb"], bd_mask_c), preferred_element_type=F32)
        pk["w"] = jnp.dot(tmb, _block_diag(pk["kbg"], bd_mask_c), preferred_element_type=F32)

    for pk in packs:
        v_new, q_s = [], []
        for j in range(HEADS_PER_PACK):
            hs = slice(j * C_DK, (j + 1) * C_DK)
            ws_qs = _bdot(jnp.concatenate([pk["w"][:, hs], pk["qg"][:, hs]], axis=0), s_ref[pk["head0"] + j])
            v_new.append(pk["u"][:, hs] - ws_qs[:cc])
            q_s.append(ws_qs[cc:])
        pk["v_new"] = v_new
        pk["q_s"] = jnp.concatenate(q_s, axis=1)

    for pk in packs:
        v_new_c = jnp.concatenate(pk["v_new"], axis=1)
        o = pk["q_s"] + jnp.dot(pk["a_qk"].astype(BF16), _block_diag(v_new_c, bd_mask_c), preferred_element_type=F32)
        pk["o_ref"][:, pk["sl"]] = o.astype(BF16)
        for j in range(HEADS_PER_PACK):
            hs = slice(j * C_DK, (j + 1) * C_DK)
            h = pk["head0"] + j
            c = pk["cols"][j]
            s_ref[h] = s_ref[h] * pk["eg_last"][:, c:c + 1] + _bdot_tn(pk["kdec"][:, hs], pk["v_new"][j])


def _delta_scan(qkv, gates, a_log, dt_bias, *, nbatch, seq, n_ctx):
    tt = qkv.shape[0]
    cc = C_CHUNK
    lc, nc = n_ctx // cc, seq // cc
    n_lat_blk = nbatch * nc

    def row_blk(reverse, b, s):
        if reverse:
            return jnp.where(s < lc, n_lat_blk + b * lc + (lc - 1 - s), b * nc + (nc - 1 - (s - lc)))
        return jnp.where(s < lc, n_lat_blk + b * lc + s, b * nc + (s - lc))

    def spec(reverse, width, col):
        return pl.BlockSpec((cc, width), lambda b, s: (row_blk(reverse, b, s), col))

    pad = jnp.zeros((1, GATE_W - 2 * C_HEADS), F32)
    alog_row = jnp.concatenate([a_log.reshape(1, 2 * C_HEADS), pad], axis=1)
    dtb_row = jnp.concatenate([dt_bias.reshape(1, 2 * C_HEADS), pad], axis=1)
    in_specs = []
    for reverse in (False, True):
        in_specs += [spec(reverse, C_W, 0), spec(reverse, C_W, 1), spec(reverse, C_W, 2), spec(reverse, GATE_W, 0)]
    in_specs += [pl.BlockSpec((1, GATE_W), lambda b, s: (0, 0))] * 2
    return pl.pallas_call(
        _delta_kernel,
        grid=(nbatch, lc + nc),
        in_specs=in_specs,
        out_specs=[spec(False, C_W, 0), spec(True, C_W, 0)],
        out_shape=[jax.ShapeDtypeStruct((tt, C_W), BF16)] * 2,
        scratch_shapes=[pltpu.VMEM((2 * C_HEADS, C_DK, C_DK), F32)],
        compiler_params=_params("arbitrary", "arbitrary"),
        name="delta_scan",
    )(qkv, qkv, qkv, gates, qkv, qkv, qkv, gates, alog_row, dtb_row)


def _gated_norm_kernel(of_ref, ob_ref, z_ref, nw_ref, o_ref):
    nw = nw_ref[...]
    for h in range(C_HEADS):
        sl = slice(h * C_DK, (h + 1) * C_DK)
        o = of_ref[:, sl].astype(F32) + ob_ref[:, sl].astype(F32)
        on = o * lax.rsqrt(jnp.mean(o * o, axis=-1, keepdims=True) + EPS) * nw
        o_ref[:, sl] = (on * _silu(z_ref[:, sl].astype(F32))).astype(o_ref.dtype)


def _gated_norm(o_f, o_b, p, norm_w):
    tt = o_f.shape[0]
    tm = _tile(512, tt)
    zcol = 3 * C_W // C_W
    return pl.pallas_call(
        _gated_norm_kernel,
        grid=(tt // tm,),
        in_specs=[pl.BlockSpec((tm, C_W), lambda i: (i, 0)),
                  pl.BlockSpec((tm, C_W), lambda i: (i, 0)),
                  pl.BlockSpec((tm, C_W), lambda i: (i, zcol)),
                  pl.BlockSpec((1, C_DK), lambda i: (0, 0))],
        out_specs=pl.BlockSpec((tm, C_W), lambda i: (i, 0)),
        out_shape=jax.ShapeDtypeStruct((tt, C_W), BF16),
        compiler_params=_params("arbitrary"),
        name="gated_norm",
    )(o_f, o_b, p, norm_w.reshape(1, C_DK))


def _max_first(vals):
    bv = vals[0]
    bi = jnp.zeros(vals[0].shape, jnp.int32)
    for idx in range(1, len(vals)):
        upd = vals[idx] > bv
        bi = jnp.where(upd, idx, bi)
        bv = jnp.where(upd, vals[idx], bv)
    return bv, bi


def _pick(vals, idx):
    out = vals[0]
    for n in range(1, len(vals)):
        out = jnp.where(idx == n, vals[n], out)
    return out


def _router_kernel(x_ref, mod_ref, nw_ref, rwt_ref, rb_ref, v_ref, route_ref):
    v = _norm_mod(x_ref[...], nw_ref[...], mod_ref[...], 3, 4)
    v_ref[...] = v
    vh, vl = _split(v)
    wh, wl = _split(rwt_ref[...])
    logits = _bdot_nt(wh, vh) + _bdot_nt(wh, vl) + _bdot_nt(wl, vh)
    scores = _sigmoid(logits)
    sel = scores + rb_ref[...]
    sel_r = [sel[e:e + 1, :] for e in range(N_EXPERTS)]
    sc_r = [scores[e:e + 1, :] for e in range(N_EXPERTS)]
    epg = EXPERTS_PER_GROUP
    gscore = []
    for g in range(N_GROUPS):
        a, b, c, d = sel_r[g * epg:(g + 1) * epg]
        m_ab, n_ab = jnp.maximum(a, b), jnp.minimum(a, b)
        m_cd, n_cd = jnp.maximum(c, d), jnp.minimum(c, d)
        top1 = jnp.maximum(m_ab, m_cd)
        top2 = jnp.maximum(jnp.minimum(m_ab, m_cd), jnp.maximum(n_ab, n_cd))
        gscore.append(top1 + top2)
    _, best = _max_first(gscore)
    in_sel = [_pick([sel_r[g * epg + k] for g in range(N_GROUPS)], best) for k in range(epg)]
    in_sc = [_pick([sc_r[g * epg + k] for g in range(N_GROUPS)], best) for k in range(epg)]
    _, i1 = _max_first(in_sel)
    _, i2 = _max_first([jnp.where(i1 == k, -jnp.inf, in_sel[k]) for k in range(epg)])
    s1 = _pick(in_sc, i1)
    s2 = _pick(in_sc, i2)
    tot = s1 + s2
    rows = [(best * epg + i1).astype(F32), (best * epg + i2).astype(F32), s1 / tot, s2 / tot]
    rows += [jnp.zeros_like(s1)] * 4
    route_ref[...] = jnp.concatenate(rows, axis=0)


def _router(xs, mod3, nw, router_w, router_b, *, n_lat, seq):
    tt, d = xs.shape
    nb = mod3.shape[0] - 1
    tm = _tile(256, seq, tt - n_lat, n_lat)
    per_seq = seq // tm
    nt = tt // tm
    return pl.pallas_call(
        _router_kernel,
        grid=(nt,),
        in_specs=[pl.BlockSpec((tm, d), lambda i: (i, 0)),
                  pl.BlockSpec((None, 6, d), lambda i: (jnp.minimum(i // per_seq, nb), 0, 0)),
                  pl.BlockSpec((1, d), lambda i: (0, 0)),
                  pl.BlockSpec((N_EXPERTS, d), lambda i: (0, 0)),
                  pl.BlockSpec((N_EXPERTS, 1), lambda i: (0, 0))],
        out_specs=[pl.BlockSpec((tm, d), lambda i: (i, 0)),
                   pl.BlockSpec((None, 8, tm), lambda i: (i, 0, 0))],
        out_shape=[jax.ShapeDtypeStruct((tt, d), F32),
                   jax.ShapeDtypeStruct((nt, 8, tm), F32)],
        compiler_params=_params("arbitrary"),
        name="router",
    )(xs, mod3, nw, router_w.T, router_b.reshape(N_EXPERTS, 1))


def _load_tile_indices(dest_hbm, idx_smem, sem, n):
    cp = pltpu.make_async_copy(dest_hbm.at[pl.ds(pl.program_id(0) * n, n)], idx_smem, sem)
    cp.start()
    cp.wait()


def _dispatch_kernel(dest_hbm, v_ref, init_hbm, xs_hbm, idx_smem, sem_idx, sem, *, tm):
    del init_hbm
    _load_tile_indices(dest_hbm, idx_smem, sem_idx, 2 * tm)

    def row_copy(t, d):
        return pltpu.make_async_copy(v_ref.at[pl.ds(t, 1)], xs_hbm.at[pl.ds(d, 1)], sem)

    def issue(t, c):
        row_copy(t, idx_smem[t]).start()
        row_copy(t, idx_smem[tm + t]).start()
        return c

    lax.fori_loop(0, tm, issue, 0, unroll=8)
    for _ in range(2):
        pltpu.make_async_copy(v_ref, xs_hbm.at[pl.ds(0, tm)], sem).wait()


def _dispatch(dest_tiles, v, xs_init, *, tm):
    tt, d = v.shape
    return pl.pallas_call(
        functools.partial(_dispatch_kernel, tm=tm),
        grid=(tt // tm,),
        in_specs=[pl.BlockSpec(memory_space=pl.ANY),
                  pl.BlockSpec((tm, d), lambda i: (i, 0)),
                  pl.BlockSpec(memory_space=pl.ANY)],
        out_specs=pl.BlockSpec(memory_space=pl.ANY),
        out_shape=jax.ShapeDtypeStruct(xs_init.shape, F32),
        scratch_shapes=[pltpu.SMEM((2 * tm,), jnp.int32), pltpu.SemaphoreType.DMA(()), pltpu.SemaphoreType.DMA(())],
        input_output_aliases={2: 0},
        compiler_params=_params("arbitrary"),
        name="moe_dispatch",
    )(dest_tiles, v, xs_init)


def _expert_kernel(te_ref, tv_ref, x_ref, wg_ref, wu_ref, wd_ref, o_ref):
    i = pl.program_id(0)

    @pl.when(tv_ref[i] > 0)
    def _():
        x = x_ref[...].astype(BF16)
        hg = jnp.dot(x, wg_ref[...], preferred_element_type=F32)
        hu = jnp.dot(x, wu_ref[...], preferred_element_type=F32)
        act = (_silu(hg) * hu).astype(BF16)
        o_ref[...] = jnp.dot(act, wd_ref[...], preferred_element_type=F32)

    @pl.when(tv_ref[i] == 0)
    def _():
        o_ref[...] = jnp.zeros(o_ref.shape, F32)


def _experts(tile_expert, tile_valid, xs, wg, wu, wd, *, tm):
    rows, d = xs.shape
    dff = wg.shape[2]
    grid_spec = pltpu.PrefetchScalarGridSpec(
        num_scalar_prefetch=2,
        grid=(rows // tm,),
        in_specs=[pl.BlockSpec((tm, d), lambda i, te, tv: (i, 0)),
                  pl.BlockSpec((None, d, dff), lambda i, te, tv: (te[i], 0, 0)),
                  pl.BlockSpec((None, d, dff), lambda i, te, tv: (te[i], 0, 0)),
                  pl.BlockSpec((None, dff, d), lambda i, te, tv: (te[i], 0, 0))],
        out_specs=pl.BlockSpec((tm, d), lambda i, te, tv: (i, 0)),
    )
    return pl.pallas_call(
        _expert_kernel,
        grid_spec=grid_spec,
        out_shape=jax.ShapeDtypeStruct((rows, d), F32),
        compiler_params=_params("arbitrary"),
        name="moe_experts",
    )(tile_expert, tile_valid, xs, wg, wu, wd)


def _combine_kernel(dest_hbm, ys_hbm, x_ref, w_ref, mod_ref, o_ref, idx_smem, ya_ref, yb_ref, sem_idx, sem, *, tm):
    _load_tile_indices(dest_hbm, idx_smem, sem_idx, 2 * tm)

    def row_copy(d, dst_ref, t):
        return pltpu.make_async_copy(ys_hbm.at[pl.ds(d, 1)], dst_ref.at[pl.ds(t, 1)], sem)

    def issue(t, c):
        row_copy(idx_smem[t], ya_ref, t).start()
        row_copy(idx_smem[tm + t], yb_ref, t).start()
        return c

    lax.fori_loop(0, tm, issue, 0, unroll=8)
    pltpu.make_async_copy(ys_hbm.at[pl.ds(0, tm)], ya_ref, sem).wait()
    pltpu.make_async_copy(ys_hbm.at[pl.ds(0, tm)], yb_ref, sem).wait()
    w = w_ref[...]
    ff = w[:, 0:1] * ya_ref[...] + w[:, 1:2] * yb_ref[...]
    o_ref[...] = x_ref[...] + mod_ref[5:6, :] * ff


def _combine(dest_tiles, ys, xs, w12, mod3, *, tm, n_lat, seq):
    tt, d = xs.shape
    nb = mod3.shape[0] - 1
    per_seq = seq // tm
    return pl.pallas_call(
        functools.partial(_combine_kernel, tm=tm),
        grid=(tt // tm,),
        in_specs=[pl.BlockSpec(memory_space=pl.ANY),
                  pl.BlockSpec(memory_space=pl.ANY),
                  pl.BlockSpec((tm, d), lambda i: (i, 0)),
                  pl.BlockSpec((tm, 2), lambda i: (i, 0)),
                  pl.BlockSpec((None, 6, d), lambda i: (jnp.minimum(i // per_seq, nb), 0, 0))],
        out_specs=pl.BlockSpec((tm, d), lambda i: (i, 0)),
        out_shape=jax.ShapeDtypeStruct((tt, d), F32),
        scratch_shapes=[pltpu.SMEM((2 * tm,), jnp.int32), pltpu.VMEM((tm, d), F32), pltpu.VMEM((tm, d), F32),
                        pltpu.SemaphoreType.DMA(()), pltpu.SemaphoreType.DMA(())],
        compiler_params=_params("arbitrary"),
        name="moe_combine",
    )(dest_tiles, ys, xs, w12, mod3)


def _moe_sorted_rows(tt):
    tme = _tile(256, tt)
    return tme, (2 * tt) // tme + N_EXPERTS


def _moe(xs, xs_sorted_init, mod3, nw, router_w, router_b, wg, wu, wd, *, n_lat, seq):
    tt, d = xs.shape
    v, route = _router(xs, mod3, nw, router_w, router_b, n_lat=n_lat, seq=seq)
    route = route.transpose(1, 0, 2).reshape(8, tt)
    e12 = route[0:2].astype(jnp.int32)
    w12 = route[2:4].T

    tme, n_tiles = _moe_sorted_rows(tt)
    flat_e = e12.reshape(-1)
    onehot = (flat_e[:, None] == jnp.arange(N_EXPERTS, dtype=jnp.int32)[None, :]).astype(jnp.int32)
    csum = jnp.cumsum(onehot, axis=0)
    rank = jnp.sum(onehot * csum, axis=1) - 1
    counts = csum[-1]
    padded = ((counts + tme - 1) // tme) * tme
    ends = jnp.cumsum(padded)
    starts = ends - padded
    dest = (jnp.sum(onehot * starts[None, :], axis=1) + rank).reshape(2, tt)
    tile_row0 = jnp.arange(n_tiles, dtype=jnp.int32) * tme
    tile_expert = jnp.minimum(jnp.sum((tile_row0[:, None] >= ends[None, :]).astype(jnp.int32), axis=1), N_EXPERTS - 1)
    tile_valid = (tile_row0 < ends[-1]).astype(jnp.int32)

    tmd = _tile(512, seq, tt - n_lat, n_lat)
    dest_tiles = dest.reshape(2, tt // tmd, tmd).transpose(1, 0, 2).reshape(-1)
    xs_sorted = _dispatch(dest_tiles, v, xs_sorted_init, tm=tmd)
    ys = _experts(tile_expert, tile_valid, xs_sorted, wg, wu, wd, tm=tme)
    return _combine(dest_tiles, ys, xs, w12, mod3, tm=tmd, n_lat=n_lat, seq=seq), xs_sorted


def _final_norm_kernel(x_ref, nw_ref, o_ref):
    x = x_ref[...]
    o_ref[...] = x * lax.rsqrt(jnp.mean(x * x, axis=-1, keepdims=True) + EPS) * nw_ref[...]


def _final_norm(xs, nw, *, n_lat):
    d = xs.shape[1]
    tm = _tile(512, n_lat)
    return pl.pallas_call(
        _final_norm_kernel,
        grid=(n_lat // tm,),
        in_specs=[pl.BlockSpec((tm, d), lambda i: (i, 0)), pl.BlockSpec((1, d), lambda i: (0, 0))],
        out_specs=pl.BlockSpec((tm, d), lambda i: (i, 0)),
        out_shape=jax.ShapeDtypeStruct((n_lat, d), F32),
        compiler_params=_params("arbitrary"),
        name="final_norm",
    )(xs, nw.reshape(1, d))


def _rope_tables(seq):
    rows = seq // GRID_W
    r = jnp.repeat(jnp.arange(rows, dtype=F32), GRID_W)
    col = jnp.tile(jnp.arange(GRID_W, dtype=F32), rows)
    n_freq = HEAD_DIM // 4
    inv = ROPE_BASE ** (-jnp.arange(n_freq, dtype=F32) / n_freq)
    ar, ac = r[:, None] * inv, col[:, None] * inv
    cos_t = jnp.concatenate([jnp.cos(ar), jnp.cos(ar), jnp.cos(ac), jnp.cos(ac)], axis=1)
    sin_t = jnp.concatenate([-jnp.sin(ar), jnp.sin(ar), -jnp.sin(ac), jnp.sin(ac)], axis=1)
    return cos_t, sin_t


def _even_layer(xs, mod3, nw, w_in, w_out, sink, rope, fconst_lat, fconst_ctx, *, nbatch, seq, n_ctx):
    n_lat = nbatch * seq
    cos_t, sin_t = rope
    p = _norm_mm(xs, mod3, nw, w_in.astype(BF16), cos_t, sin_t, n_lat=n_lat, seq=seq, rope_cols=A_Q_W + A_KV_W)
    a = _win_attn(p, sink, nbatch=nbatch, seq=seq, n_ctx=n_ctx)
    a = _ctx_attn(p, sink, a, nbatch=nbatch, n_lat=n_lat, n_ctx=n_ctx)
    f = _fourier(p, *fconst_lat, None, nbatch=nbatch, t=seq, row_blk0=0)
    f = _fourier(p, *fconst_ctx, f, nbatch=nbatch, t=n_ctx, row_blk0=n_lat // n_ctx)
    wo = w_out.astype(BF16)
    return _res_mm([a, f], [wo[:A_Q_W], wo[A_Q_W:]], xs, mod3, gate_row=2, n_lat=n_lat, seq=seq)


def _odd_layer(xs, mod3, nw, w_in, conv_w, a_log, dt_bias, norm_w, w_out, rope, *, nbatch, seq, n_ctx):
    n_lat = nbatch * seq
    d = xs.shape[1]
    cos_t, sin_t = rope
    n_main = 4 * C_W
    w_main = w_in[:, :n_main].astype(BF16)
    w_gate = jnp.concatenate([w_in[:, n_main:], jnp.zeros((d, GATE_W - 4 * C_HEADS), F32)], axis=1).astype(BF16)
    p, gates = _norm_mm(xs, mod3, nw, w_main, cos_t, sin_t, n_lat=n_lat, seq=seq, rope_cols=0, w_side=w_gate)
    qkv = _conv_prep(p, conv_w, None, nseq=nbatch, t=seq, row_blk0=0)
    qkv = _conv_prep(p, conv_w, qkv, nseq=nbatch, t=n_ctx, row_blk0=n_lat // n_ctx)
    o_f, o_b = _delta_scan(qkv, gates, a_log, dt_bias, nbatch=nbatch, seq=seq, n_ctx=n_ctx)
    on = _gated_norm(o_f, o_b, p, norm_w)
    return _res_mm([on], [w_out.astype(BF16)], xs, mod3, gate_row=2, n_lat=n_lat, seq=seq)


def kernel(x, c, ctx, c_ctx, adaln_w, adaln_b, norm_mix_w, norm_ffn_w, attn_in_w, attn_out_w, attn_sink, dn_in_w, dn_conv_w, dn_a_log, dn_dt_bias, dn_norm_w, dn_out_w, router_w, router_b, exp_gate_w, exp_up_w, exp_down_w, final_norm_w):
    nbatch, seq, d = x.shape
    n_ctx = ctx.shape[1]
    depth = adaln_w.shape[0]
    n_lat = nbatch * seq
    assert seq % BLOCK == 0 and seq % GRID_W == 0 and n_ctx % C_CHUNK == 0 and n_lat % n_ctx == 0

    xs = jnp.concatenate([x.reshape(n_lat, d), ctx.reshape(nbatch * n_ctx, d)], axis=0)
    mod_rows = 8 * ((nbatch + 1 + 7) // 8)
    c_all = jnp.concatenate([c, c_ctx[None, :], jnp.zeros((mod_rows - nbatch - 1, d), F32)], axis=0)
    mod_all = _adaln(c_all, adaln_w, adaln_b)[:, :nbatch + 1].reshape(depth, nbatch + 1, 6, d)

    rope = _rope_tables(seq)
    fconst_lat = _fourier_consts(seq)
    fconst_ctx = _fourier_consts(n_ctx)
    tme, n_tiles = _moe_sorted_rows(xs.shape[0])
    xs_sorted = jnp.zeros((n_tiles * tme, d), F32)

    for layer in range(depth):
        mod3 = mod_all[layer]
        i = layer // 2
        nw = norm_mix_w[layer].reshape(1, d)
        if layer % 2 == 0:
            xs = _even_layer(xs, mod3, nw, attn_in_w[i], attn_out_w[i], attn_sink[i], rope, fconst_lat, fconst_ctx,
                             nbatch=nbatch, seq=seq, n_ctx=n_ctx)
        else:
            xs = _odd_layer(xs, mod3, nw, dn_in_w[i], dn_conv_w[i], dn_a_log[i], dn_dt_bias[i], dn_norm_w[i],
                            dn_out_w[i], rope, nbatch=nbatch, seq=seq, n_ctx=n_ctx)
        xs, xs_sorted = _moe(xs, xs_sorted, mod3, norm_ffn_w[layer].reshape(1, d), router_w, router_b,
                             exp_gate_w[layer].astype(BF16), exp_up_w[layer].astype(BF16),
                             exp_down_w[layer].astype(BF16), n_lat=n_lat, seq=seq)
    return _final_norm(xs, final_norm_w, n_lat=n_lat).reshape(nbatch, seq, d)
```

```python
import functools
import math

import jax
import jax.numpy as jnp
from jax import lax
from jax.experimental import pallas as pl
from jax.experimental.pallas import tpu as pltpu

F32 = jnp.float32
BF16 = jnp.bfloat16
EPS = 1e-6

HEAD_DIM = 128
A_Q_HEADS = 12
A_KV_HEADS = 4
A_GROUP = A_Q_HEADS // A_KV_HEADS
BLOCK = 128
GRID_W = 64
ROPE_BASE = 10000.0
B_GROUPS = 4
B_GROUP_DIM = 128
C_HEADS = 16
C_DK = 128
C_CONV = 5
C_CHUNK = 64
N_EXPERTS = 16
N_GROUPS = 4
EXPERTS_PER_GROUP = N_EXPERTS // N_GROUPS

A_Q_W = A_Q_HEADS * HEAD_DIM
A_KV_W = A_KV_HEADS * HEAD_DIM
B_W = B_GROUPS * B_GROUP_DIM
C_W = C_HEADS * C_DK
GATE_W = 128

V7X_VMEM_LIMIT_BYTES = 56 * 1024 * 1024
NEG_BIG = -1e30


def _tile(pref, *dims, mult=8):
    t = min((pref,) + dims)
    t -= t % mult
    while t > mult and any(d % t for d in dims):
        t -= mult
    assert t >= mult and all(d % t == 0 for d in dims), (pref, dims)
    return t


def _params(*sem):
    return pltpu.CompilerParams(dimension_semantics=sem, vmem_limit_bytes=V7X_VMEM_LIMIT_BYTES)


def _bdot(a, b):
    return jnp.dot(a.astype(BF16), b.astype(BF16), preferred_element_type=F32)


def _bdot_nt(a, b):
    return lax.dot_general(a.astype(BF16), b.astype(BF16), (((1,), (1,)), ((), ())), preferred_element_type=F32)


def _bdot_tn(a, b):
    return lax.dot_general(a.astype(BF16), b.astype(BF16), (((0,), (0,)), ((), ())), preferred_element_type=F32)


def _split(a):
    hi = a.astype(BF16)
    lo = (a - hi.astype(F32)).astype(BF16)
    return hi, lo


def _sigmoid(x):
    return 1.0 / (1.0 + jnp.exp(-x))


def _silu(x):
    return x * _sigmoid(x)


def _adaln_kernel(c_ref, w_ref, b_ref, o_ref):
    sc = _silu(c_ref[...])
    sh, sl = _split(sc)
    wh, wl = _split(w_ref[...])
    acc = (jnp.dot(sh, wh, preferred_element_type=F32) + jnp.dot(sh, wl, preferred_element_type=F32)
           + jnp.dot(sl, wh, preferred_element_type=F32))
    o_ref[...] = acc + b_ref[...]


def _adaln(c_all, adaln_w, adaln_b):
    depth, d, n6 = adaln_w.shape
    rows = c_all.shape[0]
    tn = _tile(512, n6, mult=128)
    return pl.pallas_call(
        _adaln_kernel,
        grid=(depth, n6 // tn),
        in_specs=[pl.BlockSpec((rows, d), lambda l, j: (0, 0)),
                  pl.BlockSpec((None, d, tn), lambda l, j: (l, 0, j)),
                  pl.BlockSpec((None, 1, tn), lambda l, j: (l, 0, j))],
        out_specs=pl.BlockSpec((None, rows, tn), lambda l, j: (l, 0, j)),
        out_shape=jax.ShapeDtypeStruct((depth, rows, n6), F32),
        compiler_params=_params("arbitrary", "arbitrary"),
        name="adaln",
    )(c_all, adaln_w, adaln_b.reshape(depth, 1, n6))


def _norm_mod(x, nw, mod, shift_row, scale_row):
    ms = jnp.mean(x * x, axis=-1, keepdims=True)
    y = x * lax.rsqrt(ms + EPS) * nw
    return y * (1.0 + mod[scale_row:scale_row + 1, :]) + mod[shift_row:shift_row + 1, :]


def _rope_slab(a, cos, sin):
    lane = lax.broadcasted_iota(jnp.int32, a.shape, 1)
    first = (lane % 64) < 32
    partner = jnp.where(first, pltpu.roll(a, 96, 1), pltpu.roll(a, 32, 1))
    return a * cos + partner * sin


def _norm_mm_kernel(x_ref, mod_ref, nw_ref, w_ref, cos_ref, sin_ref, *rest, rope_tiles, n_lat_tiles, tn):
    if len(rest) == 4:
        w2_ref, o_ref, o2_ref, u_ref = rest
    else:
        (o_ref, u_ref), w2_ref, o2_ref = rest, None, None
    i = pl.program_id(0)
    j = pl.program_id(1)

    @pl.when(j == 0)
    def _():
        u_ref[...] = _norm_mod(x_ref[...], nw_ref[...], mod_ref[...], 0, 1).astype(BF16)
        if w2_ref is not None:
            o2_ref[...] = jnp.dot(u_ref[...], w2_ref[...], preferred_element_type=F32)

    acc = jnp.dot(u_ref[...], w_ref[...], preferred_element_type=F32)
    if rope_tiles == 0:
        o_ref[...] = acc.astype(o_ref.dtype)
    else:
        do_rope = jnp.logical_and(j < rope_tiles, i < n_lat_tiles)

        @pl.when(do_rope)
        def _():
            cos = cos_ref[...]
            sin = sin_ref[...]
            for s in range(tn // HEAD_DIM):
                sl = slice(s * HEAD_DIM, (s + 1) * HEAD_DIM)
                o_ref[:, sl] = _rope_slab(acc[:, sl], cos, sin).astype(o_ref.dtype)

        @pl.when(jnp.logical_not(do_rope))
        def _():
            o_ref[...] = acc.astype(o_ref.dtype)


def _norm_mm(xs, mod3, nw, w, cos_t, sin_t, *, n_lat, seq, rope_cols, w_side=None):
    tt, d = xs.shape
    nout = w.shape[1]
    nb = mod3.shape[0] - 1
    tm = _tile(1024, seq, tt - n_lat, n_lat)
    tn = _tile(1024, nout, *((rope_cols,) if rope_cols else ()), mult=128)
    per_seq = seq // tm
    kern = functools.partial(_norm_mm_kernel, rope_tiles=rope_cols // tn, n_lat_tiles=n_lat // tm, tn=tn)
    in_specs = [pl.BlockSpec((tm, d), lambda i, j: (i, 0)),
                pl.BlockSpec((None, 6, d), lambda i, j: (jnp.minimum(i // per_seq, nb), 0, 0)),
                pl.BlockSpec((1, d), lambda i, j: (0, 0)),
                pl.BlockSpec((d, tn), lambda i, j: (0, j)),
                pl.BlockSpec((tm, HEAD_DIM), lambda i, j: (i % per_seq, 0)),
                pl.BlockSpec((tm, HEAD_DIM), lambda i, j: (i % per_seq, 0))]
    out_specs = [pl.BlockSpec((tm, tn), lambda i, j: (i, j))]
    out_shape = [jax.ShapeDtypeStruct((tt, nout), BF16)]
    args = [xs, mod3, nw, w, cos_t, sin_t]
    if w_side is not None:
        nside = w_side.shape[1]
        in_specs.append(pl.BlockSpec((d, nside), lambda i, j: (0, 0)))
        out_specs.append(pl.BlockSpec((tm, nside), lambda i, j: (i, 0)))
        out_shape.append(jax.ShapeDtypeStruct((tt, nside), F32))
        args.append(w_side)
    outs = pl.pallas_call(
        kern,
        grid=(tt // tm, nout // tn),
        in_specs=in_specs,
        out_specs=out_specs,
        out_shape=out_shape,
        scratch_shapes=[pltpu.VMEM((tm, d), BF16)],
        compiler_params=_params("arbitrary", "arbitrary"),
        name="norm_mm",
    )(*args)
    return outs if w_side is not None else outs[0]


def _res_mm_kernel(*refs, n_parts, gate_row):
    a_refs = refs[:n_parts]
    w_refs = refs[n_parts:2 * n_parts]
    x_ref, mod_ref, o_ref = refs[2 * n_parts:]
    acc = jnp.dot(a_refs[0][...], w_refs[0][...], preferred_element_type=F32)
    for a_ref, w_ref in zip(a_refs[1:], w_refs[1:]):
        acc = acc + jnp.dot(a_ref[...], w_ref[...], preferred_element_type=F32)
    o_ref[...] = x_ref[...] + mod_ref[gate_row:gate_row + 1, :] * acc


def _res_mm(parts, weights, xs, mod3, *, gate_row, n_lat, seq):
    tt, d = xs.shape
    nb = mod3.shape[0] - 1
    tm = _tile(512, seq, tt - n_lat, n_lat)
    tn = d
    per_seq = seq // tm
    n_parts = len(parts)
    in_specs = ([pl.BlockSpec((tm, a.shape[1]), lambda i, j: (i, 0)) for a in parts]
                + [pl.BlockSpec((w.shape[0], tn), lambda i, j: (0, j)) for w in weights]
                + [pl.BlockSpec((tm, tn), lambda i, j: (i, j)),
                   pl.BlockSpec((None, 6, tn), lambda i, j: (jnp.minimum(i // per_seq, nb), 0, j))])
    return pl.pallas_call(
        functools.partial(_res_mm_kernel, n_parts=n_parts, gate_row=gate_row),
        grid=(tt // tm, d // tn),
        in_specs=in_specs,
        out_specs=pl.BlockSpec((tm, tn), lambda i, j: (i, j)),
        out_shape=jax.ShapeDtypeStruct((tt, d), F32),
        compiler_params=_params("arbitrary", "arbitrary"),
        name="res_mm",
    )(*parts, *weights, xs, mod3)


def _softmax_sink_pv(s, sk, v):
    m = jnp.maximum(jnp.max(s, axis=-1, keepdims=True), sk)
    p = jnp.exp(s - m)
    den = jnp.sum(p, axis=-1, keepdims=True) + jnp.exp(sk - m)
    return _bdot(p, v) / den


def _sink_column(sink_ref, h, rows_per_head):
    r = lax.broadcasted_iota(jnp.int32, (A_GROUP * rows_per_head, 1), 0) // rows_per_head
    sk = jnp.zeros((A_GROUP * rows_per_head, 1), F32)
    for g in range(A_GROUP):
        sk = jnp.where(r == g, sink_ref[h * A_GROUP + g], sk)
    return sk


def _stack_heads(q):
    return jnp.concatenate([q[:, g * HEAD_DIM:(g + 1) * HEAD_DIM] for g in range(A_GROUP)], axis=0)


def _unstack_heads(o, rows):
    return jnp.concatenate([o[g * rows:(g + 1) * rows, :] for g in range(A_GROUP)], axis=1)


def _win_attn_kernel(sink_ref, q_ref, km_ref, k0_ref, kp_ref, kc_ref, vm_ref, v0_ref, vp_ref, vc_ref, o_ref, *, nblk):
    i = pl.program_id(1)
    n_ctx = kc_ref.shape[0]
    nk = 3 * BLOCK + n_ctx
    r = lax.broadcasted_iota(jnp.int32, (A_GROUP * BLOCK, nk), 0) % BLOCK
    c = lax.broadcasted_iota(jnp.int32, (A_GROUP * BLOCK, nk), 1)
    band = jnp.logical_and(c >= r, c <= r + 2 * BLOCK)
    lo_ok = jnp.logical_or(c >= BLOCK, i > 0)
    hi_ok = jnp.logical_or(c < 2 * BLOCK, i < nblk - 1)
    valid = jnp.logical_or(c >= 3 * BLOCK, jnp.logical_and(band, jnp.logical_and(lo_ok, hi_ok)))
    for h in range(A_KV_HEADS):
        hs = slice(h * HEAD_DIM, (h + 1) * HEAD_DIM)
        qsl = slice(h * A_GROUP * HEAD_DIM, (h + 1) * A_GROUP * HEAD_DIM)
        qs = _stack_heads(q_ref[:, qsl])
        k = jnp.concatenate([km_ref[:, hs], k0_ref[:, hs], kp_ref[:, hs], kc_ref[:, hs]], axis=0)
        v = jnp.concatenate([vm_ref[:, hs], v0_ref[:, hs], vp_ref[:, hs], vc_ref[:, hs]], axis=0)
        s = jnp.where(valid, _bdot_nt(qs, k) * (HEAD_DIM ** -0.5), NEG_BIG)
        o = _softmax_sink_pv(s, _sink_column(sink_ref, h, BLOCK), v)
        o_ref[:, qsl] = _unstack_heads(o, BLOCK).astype(o_ref.dtype)


def _win_attn(p, sink, *, nbatch, seq, n_ctx):
    tt = p.shape[0]
    nblk = seq // BLOCK
    n_lat = nbatch * seq
    ctx_blk0 = n_lat // n_ctx
    kcol = A_Q_W // A_KV_W
    vcol = (A_Q_W + A_KV_W) // A_KV_W

    def kv_spec(col, off):
        return pl.BlockSpec((BLOCK, A_KV_W), lambda b, i, s: (b * nblk + jnp.clip(i + off, 0, nblk - 1), col))

    def ctx_spec(col):
        return pl.BlockSpec((n_ctx, A_KV_W), lambda b, i, s: (ctx_blk0 + b, col))

    grid_spec = pltpu.PrefetchScalarGridSpec(
        num_scalar_prefetch=1,
        grid=(nbatch, nblk),
        in_specs=[pl.BlockSpec((BLOCK, A_Q_W), lambda b, i, s: (b * nblk + i, 0)),
                  kv_spec(kcol, -1), kv_spec(kcol, 0), kv_spec(kcol, 1), ctx_spec(kcol),
                  kv_spec(vcol, -1), kv_spec(vcol, 0), kv_spec(vcol, 1), ctx_spec(vcol)],
        out_specs=pl.BlockSpec((BLOCK, A_Q_W), lambda b, i, s: (b * nblk + i, 0)),
    )
    return pl.pallas_call(
        functools.partial(_win_attn_kernel, nblk=nblk),
        grid_spec=grid_spec,
        out_shape=jax.ShapeDtypeStruct((tt, A_Q_W), BF16),
        compiler_params=_params("arbitrary", "arbitrary"),
        name="win_attn",
    )(sink, p, p, p, p, p, p, p, p, p)


def _ctx_attn_kernel(sink_ref, q_ref, k_ref, v_ref, prev_ref, o_ref):
    del prev_ref
    h = pl.program_id(1)
    n_ctx = q_ref.shape[0]
    s = _bdot_nt(_stack_heads(q_ref[...]), k_ref[...]) * (HEAD_DIM ** -0.5)
    o = _softmax_sink_pv(s, _sink_column(sink_ref, h, n_ctx), v_ref[...])
    o_ref[...] = _unstack_heads(o, n_ctx).astype(o_ref.dtype)


def _ctx_attn(p, sink, a_lat, *, nbatch, n_lat, n_ctx):
    ctx_blk0 = n_lat // n_ctx
    kcol = A_Q_W // HEAD_DIM
    vcol = (A_Q_W + A_KV_W) // HEAD_DIM
    grid_spec = pltpu.PrefetchScalarGridSpec(
        num_scalar_prefetch=1,
        grid=(nbatch, A_KV_HEADS),
        in_specs=[pl.BlockSpec((n_ctx, A_GROUP * HEAD_DIM), lambda b, h, s: (ctx_blk0 + b, h)),
                  pl.BlockSpec((n_ctx, HEAD_DIM), lambda b, h, s: (ctx_blk0 + b, kcol + h)),
                  pl.BlockSpec((n_ctx, HEAD_DIM), lambda b, h, s: (ctx_blk0 + b, vcol + h)),
                  pl.BlockSpec(memory_space=pl.ANY)],
        out_specs=pl.BlockSpec((n_ctx, A_GROUP * HEAD_DIM), lambda b, h, s: (ctx_blk0 + b, h)),
    )
    return pl.pallas_call(
        _ctx_attn_kernel,
        grid_spec=grid_spec,
        out_shape=jax.ShapeDtypeStruct(a_lat.shape, BF16),
        input_output_aliases={4: 0},
        compiler_params=_params("arbitrary", "arbitrary"),
        name="ctx_attn",
    )(sink, p, p, p, a_lat)


def _dft_tables(t):
    k = jnp.arange(t, dtype=jnp.int32)
    ang = ((k[:, None] * k[None, :]) % t).astype(F32) * (2.0 * math.pi / t)
    scale = t ** -0.5
    return jnp.cos(ang) * scale, jnp.sin(ang) * scale


def _fourier_kernel(f_ref, w1_ref, dft_ref, *rest):
    o_ref, stk_ref = rest[-2:]
    t = f_ref.shape[0]

    @pl.when(pl.program_id(1) == 0)
    def _():
        f1 = jnp.dot(f_ref[...], w1_ref[...], preferred_element_type=F32)
        stk_ref[0:t, :] = f1[:, :B_W].astype(BF16)
        stk_ref[t:2 * t, :] = f1[:, B_W:].astype(BF16)

    o_ref[...] = jnp.dot(dft_ref[...], stk_ref[...], preferred_element_type=F32).astype(o_ref.dtype)


def _fourier(p, w1, dft2, prev, *, nbatch, t, row_blk0):
    fcol = (A_Q_W + 2 * A_KV_W) // B_W
    tq = _tile(512, t)
    nq = t // tq
    in_specs = [pl.BlockSpec((t, B_W), lambda b, q: (row_blk0 + b, fcol)),
                pl.BlockSpec((B_W, 2 * B_W), lambda b, q: (0, 0)),
                pl.BlockSpec((tq, 2 * t), lambda b, q: (q, 0))]
    args = [p, w1, dft2]
    aliases = {}
    if prev is not None:
        in_specs.append(pl.BlockSpec(memory_space=pl.ANY))
        args.append(prev)
        aliases = {3: 0}
    return pl.pallas_call(
        _fourier_kernel,
        grid=(nbatch, nq),
        in_specs=in_specs,
        out_specs=pl.BlockSpec((tq, B_W), lambda b, q: ((row_blk0 + b) * nq + q, 0)),
        out_shape=jax.ShapeDtypeStruct((p.shape[0], B_W), BF16),
        scratch_shapes=[pltpu.VMEM((2 * t, B_W), BF16)],
        input_output_aliases=aliases,
        compiler_params=_params("arbitrary", "arbitrary"),
        name="fourier",
    )(*args)


def _fourier_consts(t):
    cd, sd = _dft_tables(B_GROUP_DIM)
    eye = jnp.eye(B_GROUPS, dtype=F32)
    w1 = jnp.concatenate([jnp.kron(eye, cd), jnp.kron(eye, sd)], axis=1).astype(BF16)
    ct, st = _dft_tables(t)
    dft2 = jnp.concatenate([ct, -st], axis=1).astype(BF16)
    return w1, dft2


def _conv_prep_kernel(p_ref, w_ref, *rest, tc):
    o_ref, pad_ref = rest[-2:]
    j = pl.program_id(1)
    t = p_ref.shape[0]
    half = C_CONV // 2
    pad_ref[0:8, :] = jnp.zeros((8, tc), F32)
    pad_ref[t + 8:t + 16, :] = jnp.zeros((8, tc), F32)
    pad_ref[8:t + 8, :] = p_ref[...].astype(F32)
    w = w_ref[...]
    is_q = j < (C_W // tc)
    is_qk = j < (2 * C_W // tc)
    post = jnp.where(is_q, C_DK ** -0.5, 1.0)
    rc = _tile(256, t)
    for r0 in range(0, t, rc):
        win = pad_ref[r0:r0 + rc + 16, :]
        acc = w[half:half + 1, :] * win[8:8 + rc, :]
        for jj in range(C_CONV):
            if jj != half:
                shifted = pltpu.roll(win, (half - jj) % (rc + 16), 0)
                acc = acc + w[jj:jj + 1, :] * shifted[8:8 + rc, :]
        y = _silu(acc)
        for hh in range(tc // C_DK):
            sl = slice(hh * C_DK, (hh + 1) * C_DK)
            yh = y[:, sl]
            nrm = lax.rsqrt(jnp.sum(yh * yh, axis=-1, keepdims=True) + EPS) * post
            o_ref[r0:r0 + rc, sl] = (yh * jnp.where(is_qk, nrm, 1.0)).astype(o_ref.dtype)


def _conv_prep(p, conv_w, prev, *, nseq, t, row_blk0):
    tc = 512
    ncol = 3 * C_W // tc
    in_specs = [pl.BlockSpec((t, tc), lambda b, j: (row_blk0 + b, j)),
                pl.BlockSpec((C_CONV, tc), lambda b, j: (0, j))]
    args = [p, conv_w]
    aliases = {}
    if prev is not None:
        in_specs.append(pl.BlockSpec(memory_space=pl.ANY))
        args.append(prev)
        aliases = {2: 0}
    return pl.pallas_call(
        functools.partial(_conv_prep_kernel, tc=tc),
        grid=(nseq, ncol),
        in_specs=in_specs,
        out_specs=pl.BlockSpec((t, tc), lambda b, j: (row_blk0 + b, j)),
        out_shape=jax.ShapeDtypeStruct((p.shape[0], 3 * C_W), BF16),
        scratch_shapes=[pltpu.VMEM((t + 16, tc), F32)],
        input_output_aliases=aliases,
        compiler_params=_params("arbitrary", "arbitrary"),
        name="conv_prep",
    )(*args)


HEADS_PER_PACK = 2
PACK_W = HEADS_PER_PACK * C_CHUNK
PACK_C = HEADS_PER_PACK * C_DK


def _split3(a):
    hi = a.astype(BF16)
    r1 = a - hi.astype(F32)
    mid = r1.astype(BF16)
    lo = (r1 - mid.astype(F32)).astype(BF16)
    return hi, mid, lo


def _dot3(m, a):
    hi, mid, lo = _split3(a)
    return (jnp.dot(m, hi, preferred_element_type=F32) + jnp.dot(m, mid, preferred_element_type=F32)
            + jnp.dot(m, lo, preferred_element_type=F32))


def _block_diag(x, mask):
    xb = x.astype(BF16)
    return jnp.where(mask, jnp.concatenate([xb] * HEADS_PER_PACK, axis=0), jnp.zeros((), BF16))


def _delta_kernel(*refs):
    ndir = 2
    in_refs = [refs[4 * s:4 * s + 4] for s in range(ndir)]
    alog_ref, dtb_ref = refs[4 * ndir:4 * ndir + 2]
    o_refs = refs[4 * ndir + 2:5 * ndir + 2]
    s_ref = refs[5 * ndir + 2]
    step = pl.program_id(1)
    cc = C_CHUNK
    npack = C_HEADS // HEADS_PER_PACK

    @pl.when(step == 0)
    def _():
        s_ref[...] = jnp.zeros(s_ref.shape, F32)

    ri = lax.broadcasted_iota(jnp.int32, (cc, cc), 0)
    ci = lax.broadcasted_iota(jnp.int32, (cc, cc), 1)
    rw = lax.broadcasted_iota(jnp.int32, (cc, PACK_W), 0)
    lw = lax.broadcasted_iota(jnp.int32, (cc, PACK_W), 1)
    cw = lw % cc
    hw = lw // cc
    dir_masks = [(ci <= ri, cw <= rw, cw < rw, cc - 1), (ci >= ri, cw >= rw, cw > rw, 0)]
    eye_w = cw == rw
    ones = jnp.ones((cc, cc), BF16)
    bd_mask_w = (lax.broadcasted_iota(jnp.int32, (PACK_W, PACK_W), 0) // cc
                 == lax.broadcasted_iota(jnp.int32, (PACK_W, PACK_W), 1) // cc)
    bd_mask_c = (lax.broadcasted_iota(jnp.int32, (PACK_W, PACK_C), 0) // cc
                 == lax.broadcasted_iota(jnp.int32, (PACK_W, PACK_C), 1) // C_DK)

    def lanes(arr, cols, width):
        return jnp.concatenate([jnp.broadcast_to(arr[:, c:c + 1], (cc, width)) for c in cols], axis=1)

    packs = []
    for si, p in [(si, p) for si in range(ndir) for p in range(npack)]:
        q_ref, k_ref, v_ref, gt_ref = in_refs[si]
        d_idx = si
        if p == 0:
            incl_sq, incl, strict, last = dir_masks[si]
            tri = jnp.where(incl_sq, 1.0, 0.0).astype(BF16)
            gt = gt_ref[...]
            x = gt + dtb_ref[...]
            softplus = jnp.maximum(x, 0.0) + jnp.log(1.0 + jnp.exp(-jnp.abs(x)))
            g_all = -jnp.exp(alog_ref[...]) * softplus
            beta_all = _sigmoid(gt)
            gc_all = _dot3(tri, g_all)
            g_last = gc_all[last:last + 1, :]
            egc_all = jnp.exp(gc_all)
            edl_all = jnp.exp(g_last - gc_all)
            eg_last = jnp.exp(g_last)
        cols = [d_idx * C_HEADS + p * HEADS_PER_PACK + j for j in range(HEADS_PER_PACK)]
        sl = slice(p * PACK_C, (p + 1) * PACK_C)
        qn, kn, vn = q_ref[:, sl], k_ref[:, sl], v_ref[:, sl]
        kf = kn.astype(F32)
        beta_b = lanes(beta_all, [c + 2 * C_HEADS for c in cols], C_DK)
        egc_b = lanes(egc_all, cols, C_DK)
        kb = kf * beta_b
        vb = vn.astype(F32) * beta_b
        kbg = kb * egc_b
        qg = qn.astype(F32) * egc_b
        kdec = kf * lanes(edl_all, cols, C_DK)
        kkqk = _bdot_nt(jnp.concatenate([kb.astype(BF16), qn], axis=0), _block_diag(kn, bd_mask_c))
        gc_w = jnp.broadcast_to(gc_all[:, cols[-1]:cols[-1] + 1], (cc, PACK_W))
        for j in range(HEADS_PER_PACK - 1):
            gc_w = jnp.where(hw == j, gc_all[:, cols[j]:cols[j] + 1], gc_w)
        gr_w = _dot3(ones, jnp.where(eye_w, gc_w, 0.0))
        decay = jnp.where(incl, jnp.exp(jnp.where(incl, gc_w - gr_w, 0.0)), 0.0)
        nmat = jnp.where(strict, kkqk[:cc] * decay, 0.0)
        a_qk = jnp.where(incl, kkqk[cc:] * decay, 0.0)
        packs.append(dict(cols=cols, sl=sl, vb=vb, kbg=kbg, qg=qg, kdec=kdec, a_qk=a_qk, eg_last=eg_last,
                          o_ref=o_refs[si], head0=si * C_HEADS + p * HEADS_PER_PACK,
                          xm=-nmat, tm=jnp.where(eye_w, 1.0, 0.0) - nmat))

    for pk in packs:
        pk["bd"] = _block_diag(pk["xm"], bd_mask_w)
    for _ in range(5):
        for pk in packs:
            pk["xm"] = jnp.dot(pk["xm"].astype(BF16), pk["bd"], preferred_element_type=F32)
        for pk in packs:
            pk["bd"] = _block_diag(pk["xm"], bd_mask_w)
        for pk in packs:
            pk["tm"] = pk["tm"] + jnp.dot(pk["tm"].astype(BF16), pk["bd"], preferred_element_type=F32)

    for pk in packs:
        tmb = pk["tm"].astype(BF16)
        pk["u"] = jnp.dot(tmb, _block_diag(pk["vb"], bd_mask_c), preferred_element_type=F32)
        pk["w"] = jnp.dot(tmb, _block_diag(pk["kbg"], bd_mask_c), preferred_element_type=F32)

    for pk in packs:
        v_new, q_s = [], []
        for j in range(HEADS_PER_PACK):
            hs = slice(j * C_DK, (j + 1) * C_DK)
            ws_qs = _bdot(jnp.concatenate([pk["w"][:, hs], pk["qg"][:, hs]], axis=0), s_ref[pk["head0"] + j])
            v_new.append(pk["u"][:, hs] - ws_qs[:cc])
            q_s.append(ws_qs[cc:])
        pk["v_new"] = v_new
        pk["q_s"] = jnp.concatenate(q_s, axis=1)

    for pk in packs:
        v_new_c = jnp.concatenate(pk["v_new"], axis=1)
        o = pk["q_s"] + jnp.dot(pk["a_qk"].astype(BF16), _block_diag(v_new_c, bd_mask_c), preferred_element_type=F32)
        pk["o_ref"][:, pk["sl"]] = o.astype(BF16)
        for j in range(HEADS_PER_PACK):
            hs = slice(j * C_DK, (j + 1) * C_DK)
            h = pk["head0"] + j
            c = pk["cols"][j]
            s_ref[h] = s_ref[h] * pk["eg_last"][:, c:c + 1] + _bdot_tn(pk["kdec"][:, hs], pk["v_new"][j])


def _delta_scan(qkv, gates, a_log, dt_bias, *, nbatch, seq, n_ctx):
    tt = qkv.shape[0]
    cc = C_CHUNK
    lc, nc = n_ctx // cc, seq // cc
    n_lat_blk = nbatch * nc

    def row_blk(reverse, b, s):
        if reverse:
            return jnp.where(s < lc, n_lat_blk + b * lc + (lc - 1 - s), b * nc + (nc - 1 - (s - lc)))
        return jnp.where(s < lc, n_lat_blk + b * lc + s, b * nc + (s - lc))

    def spec(reverse, width, col):
        return pl.BlockSpec((cc, width), lambda b, s: (row_blk(reverse, b, s), col))

    pad = jnp.zeros((1, GATE_W - 2 * C_HEADS), F32)
    alog_row = jnp.concatenate([a_log.reshape(1, 2 * C_HEADS), pad], axis=1)
    dtb_row = jnp.concatenate([dt_bias.reshape(1, 2 * C_HEADS), pad], axis=1)
    in_specs = []
    for reverse in (False, True):
        in_specs += [spec(reverse, C_W, 0), spec(reverse, C_W, 1), spec(reverse, C_W, 2), spec(reverse, GATE_W, 0)]
    in_specs += [pl.BlockSpec((1, GATE_W), lambda b, s: (0, 0))] * 2
    return pl.pallas_call(
        _delta_kernel,
        grid=(nbatch, lc + nc),
        in_specs=in_specs,
        out_specs=[spec(False, C_W, 0), spec(True, C_W, 0)],
        out_shape=[jax.ShapeDtypeStruct((tt, C_W), BF16)] * 2,
        scratch_shapes=[pltpu.VMEM((2 * C_HEADS, C_DK, C_DK), F32)],
        compiler_params=_params("arbitrary", "arbitrary"),
        name="delta_scan",
    )(qkv, qkv, qkv, gates, qkv, qkv, qkv, gates, alog_row, dtb_row)


def _gated_norm_kernel(of_ref, ob_ref, z_ref, nw_ref, o_ref):
    nw = nw_ref[...]
    for h in range(C_HEADS):
        sl = slice(h * C_DK, (h + 1) * C_DK)
        o = of_ref[:, sl].astype(F32) + ob_ref[:, sl].astype(F32)
        on = o * lax.rsqrt(jnp.mean(o * o, axis=-1, keepdims=True) + EPS) * nw
        o_ref[:, sl] = (on * _silu(z_ref[:, sl].astype(F32))).astype(o_ref.dtype)


def _gated_norm(o_f, o_b, p, norm_w):
    tt = o_f.shape[0]
    tm = _tile(512, tt)
    zcol = 3 * C_W // C_W
    return pl.pallas_call(
        _gated_norm_kernel,
        grid=(tt // tm,),
        in_specs=[pl.BlockSpec((tm, C_W), lambda i: (i, 0)),
                  pl.BlockSpec((tm, C_W), lambda i: (i, 0)),
                  pl.BlockSpec((tm, C_W), lambda i: (i, zcol)),
                  pl.BlockSpec((1, C_DK), lambda i: (0, 0))],
        out_specs=pl.BlockSpec((tm, C_W), lambda i: (i, 0)),
        out_shape=jax.ShapeDtypeStruct((tt, C_W), BF16),
        compiler_params=_params("arbitrary"),
        name="gated_norm",
    )(o_f, o_b, p, norm_w.reshape(1, C_DK))


def _max_first(vals):
    bv = vals[0]
    bi = jnp.zeros(vals[0].shape, jnp.int32)
    for idx in range(1, len(vals)):
        upd = vals[idx] > bv
        bi = jnp.where(upd, idx, bi)
        bv = jnp.where(upd, vals[idx], bv)
    return bv, bi


def _pick(vals, idx):
    out = vals[0]
    for n in range(1, len(vals)):
        out = jnp.where(idx == n, vals[n], out)
    return out


def _router_kernel(x_ref, mod_ref, nw_ref, rwt_ref, rb_ref, v_ref, route_ref):
    v = _norm_mod(x_ref[...], nw_ref[...], mod_ref[...], 3, 4)
    v_ref[...] = v
    vh, vl = _split(v)
    wh, wl = _split(rwt_ref[...])
    logits = _bdot_nt(wh, vh) + _bdot_nt(wh, vl) + _bdot_nt(wl, vh)
    scores = _sigmoid(logits)
    sel = scores + rb_ref[...]
    sel_r = [sel[e:e + 1, :] for e in range(N_EXPERTS)]
    sc_r = [scores[e:e + 1, :] for e in range(N_EXPERTS)]
    epg = EXPERTS_PER_GROUP
    gscore = []
    for g in range(N_GROUPS):
        a, b, c, d = sel_r[g * epg:(g + 1) * epg]
        m_ab, n_ab = jnp.maximum(a, b), jnp.minimum(a, b)
        m_cd, n_cd = jnp.maximum(c, d), jnp.minimum(c, d)
        top1 = jnp.maximum(m_ab, m_cd)
        top2 = jnp.maximum(jnp.minimum(m_ab, m_cd), jnp.maximum(n_ab, n_cd))
        gscore.append(top1 + top2)
    _, best = _max_first(gscore)
    in_sel = [_pick([sel_r[g * epg + k] for g in range(N_GROUPS)], best) for k in range(epg)]
    in_sc = [_pick([sc_r[g * epg + k] for g in range(N_GROUPS)], best) for k in range(epg)]
    _, i1 = _max_first(in_sel)
    _, i2 = _max_first([jnp.where(i1 == k, -jnp.inf, in_sel[k]) for k in range(epg)])
    s1 = _pick(in_sc, i1)
    s2 = _pick(in_sc, i2)
    tot = s1 + s2
    swap = i1 > i2
    e_lo = best * epg + jnp.where(swap, i2, i1)
    e_hi = best * epg + jnp.where(swap, i1, i2)
    w_lo = jnp.where(swap, s2, s1) / tot
    w_hi = jnp.where(swap, s1, s2) / tot
    rows = [e_lo.astype(F32), e_hi.astype(F32), w_lo, w_hi] + [jnp.zeros_like(s1)] * 4
    route_ref[...] = jnp.concatenate(rows, axis=0)


def _router(xs, mod3, nw, router_w, router_b, *, n_lat, seq):
    tt, d = xs.shape
    nb = mod3.shape[0] - 1
    tm = _tile(256, seq, tt - n_lat, n_lat)
    per_seq = seq // tm
    nt = tt // tm
    return pl.pallas_call(
        _router_kernel,
        grid=(nt,),
        in_specs=[pl.BlockSpec((tm, d), lambda i: (i, 0)),
                  pl.BlockSpec((None, 6, d), lambda i: (jnp.minimum(i // per_seq, nb), 0, 0)),
                  pl.BlockSpec((1, d), lambda i: (0, 0)),
                  pl.BlockSpec((N_EXPERTS, d), lambda i: (0, 0)),
                  pl.BlockSpec((N_EXPERTS, 1), lambda i: (0, 0))],
        out_specs=[pl.BlockSpec((tm, d), lambda i: (i, 0)),
                   pl.BlockSpec((None, 8, tm), lambda i: (i, 0, 0))],
        out_shape=[jax.ShapeDtypeStruct((tt, d), F32),
                   jax.ShapeDtypeStruct((nt, 8, tm), F32)],
        compiler_params=_params("arbitrary"),
        name="router",
    )(xs, mod3, nw, router_w.T, router_b.reshape(N_EXPERTS, 1))


def _load_tile_indices(dest_hbm, idx_smem, sem, n):
    cp = pltpu.make_async_copy(dest_hbm.at[pl.ds(pl.program_id(0) * n, n)], idx_smem, sem)
    cp.start()
    cp.wait()


def _dispatch_kernel(dest_hbm, v_ref, init_hbm, xs_hbm, idx_smem, sem_idx, sem, *, tm):
    del init_hbm
    _load_tile_indices(dest_hbm, idx_smem, sem_idx, idx_smem.shape[0])

    def row_copy(t, d):
        return pltpu.make_async_copy(v_ref.at[pl.ds(t, 1)], xs_hbm.at[pl.ds(d, 1)], sem)

    def issue(t, c):
        row_copy(t, idx_smem[t]).start()
        return c

    lax.fori_loop(0, tm, issue, 0, unroll=8)
    pltpu.make_async_copy(v_ref, xs_hbm.at[pl.ds(0, tm)], sem).wait()


def _dispatch(dest_tiles, v, xs_init, *, tm):
    tt, d = v.shape
    n_idx = dest_tiles.shape[0] // (tt // tm)
    return pl.pallas_call(
        functools.partial(_dispatch_kernel, tm=tm),
        grid=(tt // tm,),
        in_specs=[pl.BlockSpec(memory_space=pl.ANY),
                  pl.BlockSpec((tm, d), lambda i: (i, 0)),
                  pl.BlockSpec(memory_space=pl.ANY)],
        out_specs=pl.BlockSpec(memory_space=pl.ANY),
        out_shape=jax.ShapeDtypeStruct(xs_init.shape, F32),
        scratch_shapes=[pltpu.SMEM((n_idx,), jnp.int32), pltpu.SemaphoreType.DMA(()), pltpu.SemaphoreType.DMA(())],
        input_output_aliases={2: 0},
        compiler_params=_params("arbitrary"),
        name="moe_dispatch",
    )(dest_tiles, v, xs_init)


def _expert_kernel(wt_ref, we_ref, ws_ref, wv_ref, x_ref, wg_ref, wu_ref, wd_ref, o_ref):
    del wt_ref, we_ref, ws_ref
    n = pl.program_id(0)

    @pl.when(wv_ref[n] > 0)
    def _():
        x = x_ref[...].astype(BF16)
        hg = jnp.dot(x, wg_ref[...], preferred_element_type=F32)
        hu = jnp.dot(x, wu_ref[...], preferred_element_type=F32)
        act = (_silu(hg) * hu).astype(BF16)
        o_ref[...] = jnp.dot(act, wd_ref[...], preferred_element_type=F32)

    @pl.when(wv_ref[n] == 0)
    def _():
        o_ref[...] = jnp.zeros(o_ref.shape, F32)


def _experts(work_tile, work_expert, work_slot, work_valid, xs, wg, wu, wd, *, tm, layer):
    rows, d = xs.shape
    dff = wg.shape[3]
    grid_spec = pltpu.PrefetchScalarGridSpec(
        num_scalar_prefetch=4,
        grid=(work_tile.shape[0],),
        in_specs=[pl.BlockSpec((tm, d), lambda n, wt, we, ws, wv: (wt[n], 0)),
                  pl.BlockSpec((None, None, d, dff), lambda n, wt, we, ws, wv: (layer, we[n], 0, 0)),
                  pl.BlockSpec((None, None, d, dff), lambda n, wt, we, ws, wv: (layer, we[n], 0, 0)),
                  pl.BlockSpec((None, None, dff, d), lambda n, wt, we, ws, wv: (layer, we[n], 0, 0))],
        out_specs=pl.BlockSpec((tm, d), lambda n, wt, we, ws, wv: (wt[n], ws[n])),
    )
    return pl.pallas_call(
        _expert_kernel,
        grid_spec=grid_spec,
        out_shape=jax.ShapeDtypeStruct((rows, 2 * d), F32),
        compiler_params=_params("arbitrary"),
        name="moe_experts",
    )(work_tile, work_expert, work_slot, work_valid, xs, wg, wu, wd)


def _combine_kernel(dest_hbm, ys_hbm, x_ref, w_ref, mod_ref, o_ref, idx_smem, y_ref, sem_idx, sem, *, tm):
    _load_tile_indices(dest_hbm, idx_smem, sem_idx, idx_smem.shape[0])
    d = x_ref.shape[1]

    def issue(t, c):
        pltpu.make_async_copy(ys_hbm.at[pl.ds(idx_smem[t], 1)], y_ref.at[pl.ds(t, 1)], sem).start()
        return c

    lax.fori_loop(0, tm, issue, 0, unroll=8)
    pltpu.make_async_copy(ys_hbm.at[pl.ds(0, tm)], y_ref, sem).wait()
    w = w_ref[...]
    ff = w[:, 0:1] * y_ref[:, :d] + w[:, 1:2] * y_ref[:, d:]
    o_ref[...] = x_ref[...] + mod_ref[5:6, :] * ff


def _combine(dest_tiles, ys, xs, w12, mod3, *, tm, n_lat, seq):
    tt, d = xs.shape
    nb = mod3.shape[0] - 1
    per_seq = seq // tm
    n_idx = dest_tiles.shape[0] // (tt // tm)
    return pl.pallas_call(
        functools.partial(_combine_kernel, tm=tm),
        grid=(tt // tm,),
        in_specs=[pl.BlockSpec(memory_space=pl.ANY),
                  pl.BlockSpec(memory_space=pl.ANY),
                  pl.BlockSpec((tm, d), lambda i: (i, 0)),
                  pl.BlockSpec((tm, 2), lambda i: (i, 0)),
                  pl.BlockSpec((None, 6, d), lambda i: (jnp.minimum(i // per_seq, nb), 0, 0))],
        out_specs=pl.BlockSpec((tm, d), lambda i: (i, 0)),
        out_shape=jax.ShapeDtypeStruct((tt, d), F32),
        scratch_shapes=[pltpu.SMEM((n_idx,), jnp.int32), pltpu.VMEM((tm, 2 * d), F32),
                        pltpu.SemaphoreType.DMA(()), pltpu.SemaphoreType.DMA(())],
        compiler_params=_params("arbitrary"),
        name="moe_combine",
    )(dest_tiles, ys, xs, w12, mod3)


PAIRS_PER_GROUP = EXPERTS_PER_GROUP * (EXPERTS_PER_GROUP - 1) // 2
N_CLASSES = N_GROUPS * PAIRS_PER_GROUP
IDX_ALIGN = 1024


def _moe_sorted_rows(tt):
    tme = _tile(256, tt)
    return tme, tt // tme + N_CLASSES


def _moe(xs, xs_sorted_init, mod3, nw, router_w, router_b, wg, wu, wd, *, n_lat, seq, layer):
    tt, d = xs.shape
    v, route = _router(xs, mod3, nw, router_w, router_b, n_lat=n_lat, seq=seq)
    route = route.transpose(1, 0, 2).reshape(8, tt)
    e_lo, e_hi = route[0].astype(jnp.int32), route[1].astype(jnp.int32)
    w12 = route[2:4].T

    tme, n_tiles = _moe_sorted_rows(tt)
    epg = EXPERTS_PER_GROUP
    lo, hi = e_lo % epg, e_hi % epg
    cls = (e_lo // epg) * PAIRS_PER_GROUP + (lo * (2 * epg - 1 - lo)) // 2 + (hi - lo - 1)
    onehot = (cls[:, None] == jnp.arange(N_CLASSES, dtype=jnp.int32)[None, :]).astype(jnp.int32)
    csum = jnp.cumsum(onehot, axis=0)
    rank = jnp.sum(onehot * csum, axis=1) - 1
    cls_tiles = (csum[-1] + tme - 1) // tme
    tile_end = jnp.cumsum(cls_tiles)
    tile_start = tile_end - cls_tiles
    dest = jnp.sum(onehot * (tile_start * tme)[None, :], axis=1) + rank

    pair_lo = jnp.array([a for a in range(epg) for b in range(a + 1, epg)], jnp.int32)
    pair_hi = jnp.array([b for a in range(epg) for b in range(a + 1, epg)], jnp.int32)
    cls_group = jnp.arange(N_CLASSES, dtype=jnp.int32) // PAIRS_PER_GROUP
    cls_lo = cls_group * epg + jnp.tile(pair_lo, N_GROUPS)
    cls_hi = cls_group * epg + jnp.tile(pair_hi, N_GROUPS)
    n_work = 2 * n_tiles
    w_idx = jnp.arange(n_work, dtype=jnp.int32)
    w_cls = jnp.minimum(jnp.sum((w_idx[:, None] >= 2 * tile_end[None, :]).astype(jnp.int32), axis=1), N_CLASSES - 1)
    k = w_idx - 2 * tile_start[w_cls]
    second = (k >= cls_tiles[w_cls]).astype(jnp.int32)
    work_valid = (w_idx < 2 * tile_end[-1]).astype(jnp.int32)
    work_tile = jnp.where(work_valid > 0, tile_start[w_cls] + k - second * cls_tiles[w_cls], n_tiles - 1)
    work_slot = second * work_valid
    work_expert = jnp.where(second > 0, cls_hi[w_cls], cls_lo[w_cls])

    tmd = _tile(512, seq, tt - n_lat, n_lat)
    n_idx = IDX_ALIGN * ((tmd + IDX_ALIGN - 1) // IDX_ALIGN)
    dest_tiles = jnp.pad(dest.reshape(tt // tmd, tmd), ((0, 0), (0, n_idx - tmd))).reshape(-1)
    xs_sorted = _dispatch(dest_tiles, v, xs_sorted_init, tm=tmd)
    ys = _experts(work_tile, work_expert, work_slot, work_valid, xs_sorted, wg, wu, wd, tm=tme, layer=layer)
    return _combine(dest_tiles, ys, xs, w12, mod3, tm=tmd, n_lat=n_lat, seq=seq), xs_sorted


def _final_norm_kernel(x_ref, nw_ref, o_ref):
    x = x_ref[...]
    o_ref[...] = x * lax.rsqrt(jnp.mean(x * x, axis=-1, keepdims=True) + EPS) * nw_ref[...]


def _final_norm(xs, nw, *, n_lat):
    d = xs.shape[1]
    tm = _tile(512, n_lat)
    return pl.pallas_call(
        _final_norm_kernel,
        grid=(n_lat // tm,),
        in_specs=[pl.BlockSpec((tm, d), lambda i: (i, 0)), pl.BlockSpec((1, d), lambda i: (0, 0))],
        out_specs=pl.BlockSpec((tm, d), lambda i: (i, 0)),
        out_shape=jax.ShapeDtypeStruct((n_lat, d), F32),
        compiler_params=_params("arbitrary"),
        name="final_norm",
    )(xs, nw.reshape(1, d))


def _rope_tables(seq):
    rows = seq // GRID_W
    r = jnp.repeat(jnp.arange(rows, dtype=F32), GRID_W)
    col = jnp.tile(jnp.arange(GRID_W, dtype=F32), rows)
    n_freq = HEAD_DIM // 4
    inv = ROPE_BASE ** (-jnp.arange(n_freq, dtype=F32) / n_freq)
    ar, ac = r[:, None] * inv, col[:, None] * inv
    cos_t = jnp.concatenate([jnp.cos(ar), jnp.cos(ar), jnp.cos(ac), jnp.cos(ac)], axis=1)
    sin_t = jnp.concatenate([-jnp.sin(ar), jnp.sin(ar), -jnp.sin(ac), jnp.sin(ac)], axis=1)
    return cos_t, sin_t


def _even_layer(xs, mod3, nw, w_in, w_out, sink, rope, fconst_lat, fconst_ctx, *, nbatch, seq, n_ctx):
    n_lat = nbatch * seq
    cos_t, sin_t = rope
    p = _norm_mm(xs, mod3, nw, w_in.astype(BF16), cos_t, sin_t, n_lat=n_lat, seq=seq, rope_cols=A_Q_W + A_KV_W)
    a = _win_attn(p, sink, nbatch=nbatch, seq=seq, n_ctx=n_ctx)
    a = _ctx_attn(p, sink, a, nbatch=nbatch, n_lat=n_lat, n_ctx=n_ctx)
    f = _fourier(p, *fconst_lat, None, nbatch=nbatch, t=seq, row_blk0=0)
    f = _fourier(p, *fconst_ctx, f, nbatch=nbatch, t=n_ctx, row_blk0=n_lat // n_ctx)
    wo = w_out.astype(BF16)
    return _res_mm([a, f], [wo[:A_Q_W], wo[A_Q_W:]], xs, mod3, gate_row=2, n_lat=n_lat, seq=seq)


def _odd_layer(xs, mod3, nw, w_in, conv_w, a_log, dt_bias, norm_w, w_out, rope, *, nbatch, seq, n_ctx):
    n_lat = nbatch * seq
    d = xs.shape[1]
    cos_t, sin_t = rope
    n_main = 4 * C_W
    w_main = w_in[:, :n_main].astype(BF16)
    w_gate = jnp.concatenate([w_in[:, n_main:], jnp.zeros((d, GATE_W - 4 * C_HEADS), F32)], axis=1).astype(BF16)
    p, gates = _norm_mm(xs, mod3, nw, w_main, cos_t, sin_t, n_lat=n_lat, seq=seq, rope_cols=0, w_side=w_gate)
    qkv = _conv_prep(p, conv_w, None, nseq=nbatch, t=seq, row_blk0=0)
    qkv = _conv_prep(p, conv_w, qkv, nseq=nbatch, t=n_ctx, row_blk0=n_lat // n_ctx)
    o_f, o_b = _delta_scan(qkv, gates, a_log, dt_bias, nbatch=nbatch, seq=seq, n_ctx=n_ctx)
    on = _gated_norm(o_f, o_b, p, norm_w)
    return _res_mm([on], [w_out.astype(BF16)], xs, mod3, gate_row=2, n_lat=n_lat, seq=seq)


def kernel(x, c, ctx, c_ctx, adaln_w, adaln_b, norm_mix_w, norm_ffn_w, attn_in_w, attn_out_w, attn_sink, dn_in_w, dn_conv_w, dn_a_log, dn_dt_bias, dn_norm_w, dn_out_w, router_w, router_b, exp_gate_w, exp_up_w, exp_down_w, final_norm_w):
    nbatch, seq, d = x.shape
    n_ctx = ctx.shape[1]
    depth = adaln_w.shape[0]
    n_lat = nbatch * seq
    assert seq % BLOCK == 0 and seq % GRID_W == 0 and n_ctx % C_CHUNK == 0 and n_lat % n_ctx == 0

    xs = jnp.concatenate([x.reshape(n_lat, d), ctx.reshape(nbatch * n_ctx, d)], axis=0)
    mod_rows = 8 * ((nbatch + 1 + 7) // 8)
    c_all = jnp.concatenate([c, c_ctx[None, :], jnp.zeros((mod_rows - nbatch - 1, d), F32)], axis=0)
    mod_all = _adaln(c_all, adaln_w, adaln_b)[:, :nbatch + 1].reshape(depth, nbatch + 1, 6, d)

    rope = _rope_tables(seq)
    fconst_lat = _fourier_consts(seq)
    fconst_ctx = _fourier_consts(n_ctx)
    tme, n_tiles = _moe_sorted_rows(xs.shape[0])
    xs_sorted = jnp.zeros((n_tiles * tme, d), F32)
    wg_all, wu_all, wd_all = exp_gate_w.astype(BF16), exp_up_w.astype(BF16), exp_down_w.astype(BF16)

    for layer in range(depth):
        mod3 = mod_all[layer]
        i = layer // 2
        nw = norm_mix_w[layer].reshape(1, d)
        if layer % 2 == 0:
            xs = _even_layer(xs, mod3, nw, attn_in_w[i], attn_out_w[i], attn_sink[i], rope, fconst_lat, fconst_ctx,
                             nbatch=nbatch, seq=seq, n_ctx=n_ctx)
        else:
            xs = _odd_layer(xs, mod3, nw, dn_in_w[i], dn_conv_w[i], dn_a_log[i], dn_dt_bias[i], dn_norm_w[i],
                            dn_out_w[i], rope, nbatch=nbatch, seq=seq, n_ctx=n_ctx)
        xs, xs_sorted = _moe(xs, xs_sorted, mod3, norm_ffn_w[layer].reshape(1, d), router_w, router_b,
                             wg_all, wu_all, wd_all, n_lat=n_lat, seq=seq, layer=layer)
    return _final_norm(xs, final_norm_w, n_lat=n_lat).reshape(nbatch, seq, d)
```

```python
import functools
import math

import jax
import jax.numpy as jnp
from jax import lax
from jax.experimental import pallas as pl
from jax.experimental.pallas import tpu as pltpu

F32 = jnp.float32
BF16 = jnp.bfloat16
EPS = 1e-6

HEAD_DIM = 128
A_Q_HEADS = 12
A_KV_HEADS = 4
A_GROUP = A_Q_HEADS // A_KV_HEADS
BLOCK = 128
GRID_W = 64
ROPE_BASE = 10000.0
B_GROUPS = 4
B_GROUP_DIM = 128
C_HEADS = 16
C_DK = 128
C_CONV = 5
C_CHUNK = 64
N_EXPERTS = 16
N_GROUPS = 4
EXPERTS_PER_GROUP = N_EXPERTS // N_GROUPS

A_Q_W = A_Q_HEADS * HEAD_DIM
A_KV_W = A_KV_HEADS * HEAD_DIM
B_W = B_GROUPS * B_GROUP_DIM
C_W = C_HEADS * C_DK
GATE_W = 128

V7X_VMEM_LIMIT_BYTES = 56 * 1024 * 1024
NEG_BIG = -1e30


def _tile(pref, *dims, mult=8):
    t = min((pref,) + dims)
    t -= t % mult
    while t > mult and any(d % t for d in dims):
        t -= mult
    assert t >= mult and all(d % t == 0 for d in dims), (pref, dims)
    return t


def _row_tile(pref, seq, tt, n_lat):
    return _tile(pref, seq, n_lat, *((tt - n_lat,) if tt > n_lat else ()))


def _params(*sem):
    return pltpu.CompilerParams(dimension_semantics=sem, vmem_limit_bytes=V7X_VMEM_LIMIT_BYTES)


def _bdot(a, b):
    return jnp.dot(a.astype(BF16), b.astype(BF16), preferred_element_type=F32)


def _bdot_nt(a, b):
    return lax.dot_general(a.astype(BF16), b.astype(BF16), (((1,), (1,)), ((), ())), preferred_element_type=F32)


def _bdot_tn(a, b):
    return lax.dot_general(a.astype(BF16), b.astype(BF16), (((0,), (0,)), ((), ())), preferred_element_type=F32)


def _split(a):
    hi = a.astype(BF16)
    lo = (a - hi.astype(F32)).astype(BF16)
    return hi, lo


def _sigmoid(x):
    return 1.0 / (1.0 + jnp.exp(-x))


def _silu(x):
    return x * _sigmoid(x)


def _adaln_kernel(c_ref, w_ref, b_ref, o_ref):
    sc = _silu(c_ref[...])
    sh, sl = _split(sc)
    wh, wl = _split(w_ref[...])
    acc = (jnp.dot(sh, wh, preferred_element_type=F32) + jnp.dot(sh, wl, preferred_element_type=F32)
           + jnp.dot(sl, wh, preferred_element_type=F32))
    o_ref[...] = acc + b_ref[...]


def _adaln(c_all, adaln_w, adaln_b):
    depth, d, n6 = adaln_w.shape
    rows = c_all.shape[0]
    tn = _tile(512, n6, mult=128)
    return pl.pallas_call(
        _adaln_kernel,
        grid=(depth, n6 // tn),
        in_specs=[pl.BlockSpec((rows, d), lambda l, j: (0, 0)),
                  pl.BlockSpec((None, d, tn), lambda l, j: (l, 0, j)),
                  pl.BlockSpec((None, 1, tn), lambda l, j: (l, 0, j))],
        out_specs=pl.BlockSpec((None, rows, tn), lambda l, j: (l, 0, j)),
        out_shape=jax.ShapeDtypeStruct((depth, rows, n6), F32),
        compiler_params=_params("arbitrary", "arbitrary"),
        name="adaln",
    )(c_all, adaln_w, adaln_b.reshape(depth, 1, n6))


def _norm_mod(x, nw, mod, shift_row, scale_row):
    ms = jnp.mean(x * x, axis=-1, keepdims=True)
    y = x * lax.rsqrt(ms + EPS) * nw
    return y * (1.0 + mod[scale_row:scale_row + 1, :]) + mod[shift_row:shift_row + 1, :]


def _rope_slab(a, cos, sin):
    lane = lax.broadcasted_iota(jnp.int32, a.shape, 1)
    first = (lane % 64) < 32
    partner = jnp.where(first, pltpu.roll(a, 96, 1), pltpu.roll(a, 32, 1))
    return a * cos + partner * sin


def _norm_mm_kernel(x_ref, mod_ref, nw_ref, w_ref, cos_ref, sin_ref, *rest, rope_tiles, n_lat_tiles, tn):
    if len(rest) == 4:
        w2_ref, o_ref, o2_ref, u_ref = rest
    else:
        (o_ref, u_ref), w2_ref, o2_ref = rest, None, None
    i = pl.program_id(0)
    j = pl.program_id(1)

    @pl.when(j == 0)
    def _():
        u_ref[...] = _norm_mod(x_ref[...], nw_ref[...], mod_ref[...], 0, 1).astype(BF16)
        if w2_ref is not None:
            o2_ref[...] = jnp.dot(u_ref[...], w2_ref[...], preferred_element_type=F32)

    acc = jnp.dot(u_ref[...], w_ref[...], preferred_element_type=F32)
    if rope_tiles == 0:
        o_ref[...] = acc.astype(o_ref.dtype)
    else:
        do_rope = jnp.logical_and(j < rope_tiles, i < n_lat_tiles)

        @pl.when(do_rope)
        def _():
            cos = cos_ref[...]
            sin = sin_ref[...]
            for s in range(tn // HEAD_DIM):
                sl = slice(s * HEAD_DIM, (s + 1) * HEAD_DIM)
                o_ref[:, sl] = _rope_slab(acc[:, sl], cos, sin).astype(o_ref.dtype)

        @pl.when(jnp.logical_not(do_rope))
        def _():
            o_ref[...] = acc.astype(o_ref.dtype)


def _norm_mm(xs, mod3, nw, w, cos_t, sin_t, *, n_lat, seq, rope_cols, w_side=None):
    tt, d = xs.shape
    nout = w.shape[1]
    nb = mod3.shape[0] - 1
    tm = _row_tile(1024, seq, tt, n_lat)
    tn = _tile(1024, nout, *((rope_cols,) if rope_cols else ()), mult=128)
    per_seq = seq // tm
    kern = functools.partial(_norm_mm_kernel, rope_tiles=rope_cols // tn, n_lat_tiles=n_lat // tm, tn=tn)
    in_specs = [pl.BlockSpec((tm, d), lambda i, j: (i, 0)),
                pl.BlockSpec((None, 6, d), lambda i, j: (jnp.minimum(i // per_seq, nb), 0, 0)),
                pl.BlockSpec((1, d), lambda i, j: (0, 0)),
                pl.BlockSpec((d, tn), lambda i, j: (0, j)),
                pl.BlockSpec((tm, HEAD_DIM), lambda i, j: (i % per_seq, 0)),
                pl.BlockSpec((tm, HEAD_DIM), lambda i, j: (i % per_seq, 0))]
    out_specs = [pl.BlockSpec((tm, tn), lambda i, j: (i, j))]
    out_shape = [jax.ShapeDtypeStruct((tt, nout), BF16)]
    args = [xs, mod3, nw, w, cos_t, sin_t]
    if w_side is not None:
        nside = w_side.shape[1]
        in_specs.append(pl.BlockSpec((d, nside), lambda i, j: (0, 0)))
        out_specs.append(pl.BlockSpec((tm, nside), lambda i, j: (i, 0)))
        out_shape.append(jax.ShapeDtypeStruct((tt, nside), F32))
        args.append(w_side)
    outs = pl.pallas_call(
        kern,
        grid=(tt // tm, nout // tn),
        in_specs=in_specs,
        out_specs=out_specs,
        out_shape=out_shape,
        scratch_shapes=[pltpu.VMEM((tm, d), BF16)],
        compiler_params=_params("arbitrary", "arbitrary"),
        name="norm_mm",
    )(*args)
    return outs if w_side is not None else outs[0]


def _res_mm_kernel(*refs, n_parts, gate_row):
    a_refs = refs[:n_parts]
    w_refs = refs[n_parts:2 * n_parts]
    x_ref, mod_ref, o_ref = refs[2 * n_parts:]
    acc = jnp.dot(a_refs[0][...], w_refs[0][...], preferred_element_type=F32)
    for a_ref, w_ref in zip(a_refs[1:], w_refs[1:]):
        acc = acc + jnp.dot(a_ref[...], w_ref[...], preferred_element_type=F32)
    o_ref[...] = x_ref[...] + mod_ref[gate_row:gate_row + 1, :] * acc


def _res_mm(parts, weights, xs, mod3, *, gate_row, n_lat, seq, n_out_rows):
    tt, d = n_out_rows, xs.shape[1]
    nb = mod3.shape[0] - 1
    tm = _row_tile(512, seq, tt, n_lat)
    tn = d
    per_seq = seq // tm
    n_parts = len(parts)
    in_specs = ([pl.BlockSpec((tm, a.shape[1]), lambda i, j: (i, 0)) for a in parts]
                + [pl.BlockSpec((w.shape[0], tn), lambda i, j: (0, j)) for w in weights]
                + [pl.BlockSpec((tm, tn), lambda i, j: (i, j)),
                   pl.BlockSpec((None, 6, tn), lambda i, j: (jnp.minimum(i // per_seq, nb), 0, j))])
    return pl.pallas_call(
        functools.partial(_res_mm_kernel, n_parts=n_parts, gate_row=gate_row),
        grid=(tt // tm, d // tn),
        in_specs=in_specs,
        out_specs=pl.BlockSpec((tm, tn), lambda i, j: (i, j)),
        out_shape=jax.ShapeDtypeStruct((tt, d), F32),
        compiler_params=_params("arbitrary", "arbitrary"),
        name="res_mm",
    )(*parts, *weights, xs, mod3)


def _softmax_sink_pv(s, sk, v):
    m = jnp.maximum(jnp.max(s, axis=-1, keepdims=True), sk)
    p = jnp.exp(s - m)
    den = jnp.sum(p, axis=-1, keepdims=True) + jnp.exp(sk - m)
    return _bdot(p, v) / den


def _sink_column(sink_ref, h, rows_per_head):
    r = lax.broadcasted_iota(jnp.int32, (A_GROUP * rows_per_head, 1), 0) // rows_per_head
    sk = jnp.zeros((A_GROUP * rows_per_head, 1), F32)
    for g in range(A_GROUP):
        sk = jnp.where(r == g, sink_ref[h * A_GROUP + g], sk)
    return sk


def _stack_heads(q):
    return jnp.concatenate([q[:, g * HEAD_DIM:(g + 1) * HEAD_DIM] for g in range(A_GROUP)], axis=0)


def _unstack_heads(o, rows):
    return jnp.concatenate([o[g * rows:(g + 1) * rows, :] for g in range(A_GROUP)], axis=1)


def _win_attn_kernel(sink_ref, q_ref, km_ref, k0_ref, kp_ref, kc_ref, vm_ref, v0_ref, vp_ref, vc_ref, o_ref, *, nblk):
    i = pl.program_id(1)
    n_ctx = kc_ref.shape[0]
    nk = 3 * BLOCK + n_ctx
    r = lax.broadcasted_iota(jnp.int32, (A_GROUP * BLOCK, nk), 0) % BLOCK
    c = lax.broadcasted_iota(jnp.int32, (A_GROUP * BLOCK, nk), 1)
    band = jnp.logical_and(c >= r, c <= r + 2 * BLOCK)
    lo_ok = jnp.logical_or(c >= BLOCK, i > 0)
    hi_ok = jnp.logical_or(c < 2 * BLOCK, i < nblk - 1)
    valid = jnp.logical_or(c >= 3 * BLOCK, jnp.logical_and(band, jnp.logical_and(lo_ok, hi_ok)))
    for h in range(A_KV_HEADS):
        hs = slice(h * HEAD_DIM, (h + 1) * HEAD_DIM)
        qsl = slice(h * A_GROUP * HEAD_DIM, (h + 1) * A_GROUP * HEAD_DIM)
        qs = _stack_heads(q_ref[:, qsl])
        k = jnp.concatenate([km_ref[:, hs], k0_ref[:, hs], kp_ref[:, hs], kc_ref[:, hs]], axis=0)
        v = jnp.concatenate([vm_ref[:, hs], v0_ref[:, hs], vp_ref[:, hs], vc_ref[:, hs]], axis=0)
        s = jnp.where(valid, _bdot_nt(qs, k) * (HEAD_DIM ** -0.5), NEG_BIG)
        o = _softmax_sink_pv(s, _sink_column(sink_ref, h, BLOCK), v)
        o_ref[:, qsl] = _unstack_heads(o, BLOCK).astype(o_ref.dtype)


def _win_attn(p, sink, *, nbatch, seq, n_ctx):
    tt = p.shape[0]
    nblk = seq // BLOCK
    n_lat = nbatch * seq
    ctx_blk0 = n_lat // n_ctx
    kcol = A_Q_W // A_KV_W
    vcol = (A_Q_W + A_KV_W) // A_KV_W

    def kv_spec(col, off):
        return pl.BlockSpec((BLOCK, A_KV_W), lambda b, i, s: (b * nblk + jnp.clip(i + off, 0, nblk - 1), col))

    def ctx_spec(col):
        return pl.BlockSpec((n_ctx, A_KV_W), lambda b, i, s: (ctx_blk0 + b, col))

    grid_spec = pltpu.PrefetchScalarGridSpec(
        num_scalar_prefetch=1,
        grid=(nbatch, nblk),
        in_specs=[pl.BlockSpec((BLOCK, A_Q_W), lambda b, i, s: (b * nblk + i, 0)),
                  kv_spec(kcol, -1), kv_spec(kcol, 0), kv_spec(kcol, 1), ctx_spec(kcol),
                  kv_spec(vcol, -1), kv_spec(vcol, 0), kv_spec(vcol, 1), ctx_spec(vcol)],
        out_specs=pl.BlockSpec((BLOCK, A_Q_W), lambda b, i, s: (b * nblk + i, 0)),
    )
    return pl.pallas_call(
        functools.partial(_win_attn_kernel, nblk=nblk),
        grid_spec=grid_spec,
        out_shape=jax.ShapeDtypeStruct((tt, A_Q_W), BF16),
        compiler_params=_params("arbitrary", "arbitrary"),
        name="win_attn",
    )(sink, p, p, p, p, p, p, p, p, p)


def _ctx_attn_kernel(sink_ref, q_ref, k_ref, v_ref, prev_ref, o_ref):
    del prev_ref
    h = pl.program_id(1)
    n_ctx = q_ref.shape[0]
    s = _bdot_nt(_stack_heads(q_ref[...]), k_ref[...]) * (HEAD_DIM ** -0.5)
    o = _softmax_sink_pv(s, _sink_column(sink_ref, h, n_ctx), v_ref[...])
    o_ref[...] = _unstack_heads(o, n_ctx).astype(o_ref.dtype)


def _ctx_attn(p, sink, a_lat, *, nbatch, n_lat, n_ctx):
    ctx_blk0 = n_lat // n_ctx
    kcol = A_Q_W // HEAD_DIM
    vcol = (A_Q_W + A_KV_W) // HEAD_DIM
    grid_spec = pltpu.PrefetchScalarGridSpec(
        num_scalar_prefetch=1,
        grid=(nbatch, A_KV_HEADS),
        in_specs=[pl.BlockSpec((n_ctx, A_GROUP * HEAD_DIM), lambda b, h, s: (ctx_blk0 + b, h)),
                  pl.BlockSpec((n_ctx, HEAD_DIM), lambda b, h, s: (ctx_blk0 + b, kcol + h)),
                  pl.BlockSpec((n_ctx, HEAD_DIM), lambda b, h, s: (ctx_blk0 + b, vcol + h)),
                  pl.BlockSpec(memory_space=pl.ANY)],
        out_specs=pl.BlockSpec((n_ctx, A_GROUP * HEAD_DIM), lambda b, h, s: (ctx_blk0 + b, h)),
    )
    return pl.pallas_call(
        _ctx_attn_kernel,
        grid_spec=grid_spec,
        out_shape=jax.ShapeDtypeStruct(a_lat.shape, BF16),
        input_output_aliases={4: 0},
        compiler_params=_params("arbitrary", "arbitrary"),
        name="ctx_attn",
    )(sink, p, p, p, a_lat)


def _dft_tables(t):
    k = jnp.arange(t, dtype=jnp.int32)
    ang = ((k[:, None] * k[None, :]) % t).astype(F32) * (2.0 * math.pi / t)
    scale = t ** -0.5
    return jnp.cos(ang) * scale, jnp.sin(ang) * scale


def _fourier_kernel(f_ref, w1_ref, dft_ref, *rest):
    o_ref, stk_ref = rest[-2:]
    t = f_ref.shape[0]

    @pl.when(pl.program_id(1) == 0)
    def _():
        f1 = jnp.dot(f_ref[...], w1_ref[...], preferred_element_type=F32)
        stk_ref[0:t, :] = f1[:, :B_W].astype(BF16)
        stk_ref[t:2 * t, :] = f1[:, B_W:].astype(BF16)

    o_ref[...] = jnp.dot(dft_ref[...], stk_ref[...], preferred_element_type=F32).astype(o_ref.dtype)


def _fourier(p, w1, dft2, prev, *, nbatch, t, row_blk0):
    fcol = (A_Q_W + 2 * A_KV_W) // B_W
    tq = _tile(512, t)
    nq = t // tq
    in_specs = [pl.BlockSpec((t, B_W), lambda b, q: (row_blk0 + b, fcol)),
                pl.BlockSpec((B_W, 2 * B_W), lambda b, q: (0, 0)),
                pl.BlockSpec((tq, 2 * t), lambda b, q: (q, 0))]
    args = [p, w1, dft2]
    aliases = {}
    if prev is not None:
        in_specs.append(pl.BlockSpec(memory_space=pl.ANY))
        args.append(prev)
        aliases = {3: 0}
    return pl.pallas_call(
        _fourier_kernel,
        grid=(nbatch, nq),
        in_specs=in_specs,
        out_specs=pl.BlockSpec((tq, B_W), lambda b, q: ((row_blk0 + b) * nq + q, 0)),
        out_shape=jax.ShapeDtypeStruct((p.shape[0], B_W), BF16),
        scratch_shapes=[pltpu.VMEM((2 * t, B_W), BF16)],
        input_output_aliases=aliases,
        compiler_params=_params("arbitrary", "arbitrary"),
        name="fourier",
    )(*args)


def _fourier_consts(t):
    cd, sd = _dft_tables(B_GROUP_DIM)
    eye = jnp.eye(B_GROUPS, dtype=F32)
    w1 = jnp.concatenate([jnp.kron(eye, cd), jnp.kron(eye, sd)], axis=1).astype(BF16)
    ct, st = _dft_tables(t)
    dft2 = jnp.concatenate([ct, -st], axis=1).astype(BF16)
    return w1, dft2


def _conv_prep_kernel(p_ref, w_ref, *rest, tc):
    o_ref, pad_ref = rest[-2:]
    j = pl.program_id(1)
    t = p_ref.shape[0]
    half = C_CONV // 2
    pad_ref[0:8, :] = jnp.zeros((8, tc), F32)
    pad_ref[t + 8:t + 16, :] = jnp.zeros((8, tc), F32)
    pad_ref[8:t + 8, :] = p_ref[...].astype(F32)
    w = w_ref[...]
    is_q = j < (C_W // tc)
    is_qk = j < (2 * C_W // tc)
    post = jnp.where(is_q, C_DK ** -0.5, 1.0)
    rc = _tile(256, t)
    for r0 in range(0, t, rc):
        win = pad_ref[r0:r0 + rc + 16, :]
        acc = w[half:half + 1, :] * win[8:8 + rc, :]
        for jj in range(C_CONV):
            if jj != half:
                shifted = pltpu.roll(win, (half - jj) % (rc + 16), 0)
                acc = acc + w[jj:jj + 1, :] * shifted[8:8 + rc, :]
        y = _silu(acc)
        for hh in range(tc // C_DK):
            sl = slice(hh * C_DK, (hh + 1) * C_DK)
            yh = y[:, sl]
            nrm = lax.rsqrt(jnp.sum(yh * yh, axis=-1, keepdims=True) + EPS) * post
            o_ref[r0:r0 + rc, sl] = (yh * jnp.where(is_qk, nrm, 1.0)).astype(o_ref.dtype)


def _conv_prep(p, conv_w, prev, *, nseq, t, row_blk0):
    tc = 512
    ncol = 3 * C_W // tc
    in_specs = [pl.BlockSpec((t, tc), lambda b, j: (row_blk0 + b, j)),
                pl.BlockSpec((C_CONV, tc), lambda b, j: (0, j))]
    args = [p, conv_w]
    aliases = {}
    if prev is not None:
        in_specs.append(pl.BlockSpec(memory_space=pl.ANY))
        args.append(prev)
        aliases = {2: 0}
    return pl.pallas_call(
        functools.partial(_conv_prep_kernel, tc=tc),
        grid=(nseq, ncol),
        in_specs=in_specs,
        out_specs=pl.BlockSpec((t, tc), lambda b, j: (row_blk0 + b, j)),
        out_shape=jax.ShapeDtypeStruct((p.shape[0], 3 * C_W), BF16),
        scratch_shapes=[pltpu.VMEM((t + 16, tc), F32)],
        input_output_aliases=aliases,
        compiler_params=_params("arbitrary", "arbitrary"),
        name="conv_prep",
    )(*args)


HEADS_PER_PACK = 2
PACK_W = HEADS_PER_PACK * C_CHUNK
PACK_C = HEADS_PER_PACK * C_DK


def _split3(a):
    hi = a.astype(BF16)
    r1 = a - hi.astype(F32)
    mid = r1.astype(BF16)
    lo = (r1 - mid.astype(F32)).astype(BF16)
    return hi, mid, lo


def _dot3(m, a):
    hi, mid, lo = _split3(a)
    return (jnp.dot(m, hi, preferred_element_type=F32) + jnp.dot(m, mid, preferred_element_type=F32)
            + jnp.dot(m, lo, preferred_element_type=F32))


def _block_diag(x, mask):
    xb = x.astype(BF16)
    return jnp.where(mask, jnp.concatenate([xb] * HEADS_PER_PACK, axis=0), jnp.zeros((), BF16))


def _delta_kernel(*refs):
    ndir = 2
    in_refs = [refs[4 * s:4 * s + 4] for s in range(ndir)]
    alog_ref, dtb_ref = refs[4 * ndir:4 * ndir + 2]
    o_refs = refs[4 * ndir + 2:5 * ndir + 2]
    s_ref = refs[5 * ndir + 2]
    step = pl.program_id(1)
    cc = C_CHUNK
    npack = C_HEADS // HEADS_PER_PACK

    @pl.when(step == 0)
    def _():
        s_ref[...] = jnp.zeros(s_ref.shape, F32)

    ri = lax.broadcasted_iota(jnp.int32, (cc, cc), 0)
    ci = lax.broadcasted_iota(jnp.int32, (cc, cc), 1)
    rw = lax.broadcasted_iota(jnp.int32, (cc, PACK_W), 0)
    lw = lax.broadcasted_iota(jnp.int32, (cc, PACK_W), 1)
    cw = lw % cc
    hw = lw // cc
    dir_masks = [(ci <= ri, cw <= rw, cw < rw, cc - 1), (ci >= ri, cw >= rw, cw > rw, 0)]
    eye_w = cw == rw
    bd_mask_w = (lax.broadcasted_iota(jnp.int32, (PACK_W, PACK_W), 0) // cc
                 == lax.broadcasted_iota(jnp.int32, (PACK_W, PACK_W), 1) // cc)
    bd_mask_c = (lax.broadcasted_iota(jnp.int32, (PACK_W, PACK_C), 0) // cc
                 == lax.broadcasted_iota(jnp.int32, (PACK_W, PACK_C), 1) // C_DK)

    def lanes(arr, cols, width):
        return jnp.concatenate([jnp.broadcast_to(arr[:, c:c + 1], (cc, width)) for c in cols], axis=1)

    packs = []
    for si, p in [(si, p) for si in range(ndir) for p in range(npack)]:
        q_ref, k_ref, v_ref, gt_ref = in_refs[si]
        d_idx = si
        if p == 0:
            incl_sq, incl, strict, last = dir_masks[si]
            tri = jnp.where(incl_sq, 1.0, 0.0).astype(BF16)
            gt = gt_ref[...]
            x = gt + dtb_ref[...]
            softplus = jnp.maximum(x, 0.0) + jnp.log(1.0 + jnp.exp(-jnp.abs(x)))
            g_all = -jnp.exp(alog_ref[...]) * softplus
            beta_all = _sigmoid(gt)
            gc_all = _dot3(tri, g_all)
            g_last = gc_all[last:last + 1, :]
            gc_t = jnp.concatenate([gc_all] * HEADS_PER_PACK, axis=0).T
            egc_all = jnp.exp(gc_all)
            edl_all = jnp.exp(g_last - gc_all)
            eg_last = jnp.exp(g_last)
        cols = [d_idx * C_HEADS + p * HEADS_PER_PACK + j for j in range(HEADS_PER_PACK)]
        sl = slice(p * PACK_C, (p + 1) * PACK_C)
        qn, kn, vn = q_ref[:, sl], k_ref[:, sl], v_ref[:, sl]
        kf = kn.astype(F32)
        beta_b = lanes(beta_all, [c + 2 * C_HEADS for c in cols], C_DK)
        egc_b = lanes(egc_all, cols, C_DK)
        kb = kf * beta_b
        vb = vn.astype(F32) * beta_b
        kbg = kb * egc_b
        qg = qn.astype(F32) * egc_b
        kdec = kf * lanes(edl_all, cols, C_DK)
        kkqk = _bdot_nt(jnp.concatenate([kb.astype(BF16), qn], axis=0), _block_diag(kn, bd_mask_c))
        gc_w = jnp.broadcast_to(gc_all[:, cols[-1]:cols[-1] + 1], (cc, PACK_W))
        for j in range(HEADS_PER_PACK - 1):
            gc_w = jnp.where(hw == j, gc_all[:, cols[j]:cols[j] + 1], gc_w)
        gr_w = jnp.broadcast_to(gc_t[cols[-1]:cols[-1] + 1, :], (cc, PACK_W))
        for j in range(HEADS_PER_PACK - 1):
            gr_w = jnp.where(hw == j, gc_t[cols[j]:cols[j] + 1, :], gr_w)
        decay = jnp.where(incl, jnp.exp(jnp.where(incl, gc_w - gr_w, 0.0)), 0.0)
        nmat = jnp.where(strict, kkqk[:cc] * decay, 0.0)
        a_qk = jnp.where(incl, kkqk[cc:] * decay, 0.0)
        packs.append(dict(cols=cols, sl=sl, vb=vb, kbg=kbg, qg=qg, kdec=kdec, a_qk=a_qk, eg_last=eg_last,
                          o_ref=o_refs[si], head0=si * C_HEADS + p * HEADS_PER_PACK,
                          xm=-nmat, tm=jnp.where(eye_w, 1.0, 0.0) - nmat))

    for pk in packs:
        pk["bd"] = _block_diag(pk["xm"], bd_mask_w)
    for _ in range(5):
        for pk in packs:
            pk["xm"] = jnp.dot(pk["xm"].astype(BF16), pk["bd"], preferred_element_type=F32)
        for pk in packs:
            pk["bd"] = _block_diag(pk["xm"], bd_mask_w)
        for pk in packs:
            pk["tm"] = pk["tm"] + jnp.dot(pk["tm"].astype(BF16), pk["bd"], preferred_element_type=F32)

    for pk in packs:
        tmb = pk["tm"].astype(BF16)
        pk["u"] = jnp.dot(tmb, _block_diag(pk["vb"], bd_mask_c), preferred_element_type=F32)
        pk["w"] = jnp.dot(tmb, _block_diag(pk["kbg"], bd_mask_c), preferred_element_type=F32)

    for pk in packs:
        v_new, q_s = [], []
        for j in range(HEADS_PER_PACK):
            hs = slice(j * C_DK, (j + 1) * C_DK)
            ws_qs = _bdot(jnp.concatenate([pk["w"][:, hs], pk["qg"][:, hs]], axis=0), s_ref[pk["head0"] + j])
            v_new.append(pk["u"][:, hs] - ws_qs[:cc])
            q_s.append(ws_qs[cc:])
        pk["v_new"] = v_new
        pk["q_s"] = jnp.concatenate(q_s, axis=1)

    for pk in packs:
        v_new_c = jnp.concatenate(pk["v_new"], axis=1)
        o = pk["q_s"] + jnp.dot(pk["a_qk"].astype(BF16), _block_diag(v_new_c, bd_mask_c), preferred_element_type=F32)
        pk["o_ref"][:, pk["sl"]] = o.astype(BF16)
        for j in range(HEADS_PER_PACK):
            hs = slice(j * C_DK, (j + 1) * C_DK)
            h = pk["head0"] + j
            c = pk["cols"][j]
            s_ref[h] = s_ref[h] * pk["eg_last"][:, c:c + 1] + _bdot_tn(pk["kdec"][:, hs], pk["v_new"][j])


def _delta_scan(qkv, gates, a_log, dt_bias, *, nbatch, seq, n_ctx):
    tt = qkv.shape[0]
    cc = C_CHUNK
    lc, nc = n_ctx // cc, seq // cc
    n_lat_blk = nbatch * nc

    def row_blk(reverse, b, s):
        if reverse:
            return jnp.where(s < lc, n_lat_blk + b * lc + (lc - 1 - s), b * nc + (nc - 1 - (s - lc)))
        return jnp.where(s < lc, n_lat_blk + b * lc + s, b * nc + (s - lc))

    def spec(reverse, width, col):
        return pl.BlockSpec((cc, width), lambda b, s: (row_blk(reverse, b, s), col))

    pad = jnp.zeros((1, GATE_W - 2 * C_HEADS), F32)
    alog_row = jnp.concatenate([a_log.reshape(1, 2 * C_HEADS), pad], axis=1)
    dtb_row = jnp.concatenate([dt_bias.reshape(1, 2 * C_HEADS), pad], axis=1)
    in_specs = []
    for reverse in (False, True):
        in_specs += [spec(reverse, C_W, 0), spec(reverse, C_W, 1), spec(reverse, C_W, 2), spec(reverse, GATE_W, 0)]
    in_specs += [pl.BlockSpec((1, GATE_W), lambda b, s: (0, 0))] * 2
    return pl.pallas_call(
        _delta_kernel,
        grid=(nbatch, lc + nc),
        in_specs=in_specs,
        out_specs=[spec(False, C_W, 0), spec(True, C_W, 0)],
        out_shape=[jax.ShapeDtypeStruct((tt, C_W), BF16)] * 2,
        scratch_shapes=[pltpu.VMEM((2 * C_HEADS, C_DK, C_DK), F32)],
        compiler_params=_params("arbitrary", "arbitrary"),
        name="delta_scan",
    )(qkv, qkv, qkv, gates, qkv, qkv, qkv, gates, alog_row, dtb_row)


def _gated_norm_kernel(of_ref, ob_ref, z_ref, nw_ref, o_ref):
    nw = nw_ref[...]
    for h in range(C_HEADS):
        sl = slice(h * C_DK, (h + 1) * C_DK)
        o = of_ref[:, sl].astype(F32) + ob_ref[:, sl].astype(F32)
        on = o * lax.rsqrt(jnp.mean(o * o, axis=-1, keepdims=True) + EPS) * nw
        o_ref[:, sl] = (on * _silu(z_ref[:, sl].astype(F32))).astype(o_ref.dtype)


def _gated_norm(o_f, o_b, p, norm_w):
    tt = o_f.shape[0]
    tm = _tile(512, tt)
    zcol = 3 * C_W // C_W
    return pl.pallas_call(
        _gated_norm_kernel,
        grid=(tt // tm,),
        in_specs=[pl.BlockSpec((tm, C_W), lambda i: (i, 0)),
                  pl.BlockSpec((tm, C_W), lambda i: (i, 0)),
                  pl.BlockSpec((tm, C_W), lambda i: (i, zcol)),
                  pl.BlockSpec((1, C_DK), lambda i: (0, 0))],
        out_specs=pl.BlockSpec((tm, C_W), lambda i: (i, 0)),
        out_shape=jax.ShapeDtypeStruct((tt, C_W), BF16),
        compiler_params=_params("arbitrary"),
        name="gated_norm",
    )(o_f, o_b, p, norm_w.reshape(1, C_DK))


def _max_first(vals):
    bv = vals[0]
    bi = jnp.zeros(vals[0].shape, jnp.int32)
    for idx in range(1, len(vals)):
        upd = vals[idx] > bv
        bi = jnp.where(upd, idx, bi)
        bv = jnp.where(upd, vals[idx], bv)
    return bv, bi


def _pick(vals, idx):
    out = vals[0]
    for n in range(1, len(vals)):
        out = jnp.where(idx == n, vals[n], out)
    return out


def _router_kernel(x_ref, mod_ref, nw_ref, rwt_ref, rb_ref, route_ref):
    v = _norm_mod(x_ref[...], nw_ref[...], mod_ref[...], 3, 4)
    vh, vl = _split(v)
    wh, wl = _split(rwt_ref[...])
    logits = _bdot_nt(wh, vh) + _bdot_nt(wh, vl) + _bdot_nt(wl, vh)
    scores = _sigmoid(logits)
    sel = scores + rb_ref[...]
    sel_r = [sel[e:e + 1, :] for e in range(N_EXPERTS)]
    sc_r = [scores[e:e + 1, :] for e in range(N_EXPERTS)]
    epg = EXPERTS_PER_GROUP
    gscore = []
    for g in range(N_GROUPS):
        a, b, c, d = sel_r[g * epg:(g + 1) * epg]
        m_ab, n_ab = jnp.maximum(a, b), jnp.minimum(a, b)
        m_cd, n_cd = jnp.maximum(c, d), jnp.minimum(c, d)
        top1 = jnp.maximum(m_ab, m_cd)
        top2 = jnp.maximum(jnp.minimum(m_ab, m_cd), jnp.maximum(n_ab, n_cd))
        gscore.append(top1 + top2)
    _, best = _max_first(gscore)
    in_sel = [_pick([sel_r[g * epg + k] for g in range(N_GROUPS)], best) for k in range(epg)]
    in_sc = [_pick([sc_r[g * epg + k] for g in range(N_GROUPS)], best) for k in range(epg)]
    _, i1 = _max_first(in_sel)
    _, i2 = _max_first([jnp.where(i1 == k, -jnp.inf, in_sel[k]) for k in range(epg)])
    s1 = _pick(in_sc, i1)
    s2 = _pick(in_sc, i2)
    tot = s1 + s2
    swap = i1 > i2
    e_lo = best * epg + jnp.where(swap, i2, i1)
    e_hi = best * epg + jnp.where(swap, i1, i2)
    w_lo = jnp.where(swap, s2, s1) / tot
    w_hi = jnp.where(swap, s1, s2) / tot
    rows = [e_lo.astype(F32), e_hi.astype(F32), w_lo, w_hi] + [jnp.zeros_like(s1)] * 4
    route_ref[...] = jnp.concatenate(rows, axis=0)


def _router(xs, mod3, nw, router_w, router_b, *, n_lat, seq):
    tt, d = xs.shape
    nb = mod3.shape[0] - 1
    tm = _row_tile(256, seq, tt, n_lat)
    per_seq = seq // tm
    nt = tt // tm
    return pl.pallas_call(
        _router_kernel,
        grid=(nt,),
        in_specs=[pl.BlockSpec((tm, d), lambda i: (i, 0)),
                  pl.BlockSpec((None, 6, d), lambda i: (jnp.minimum(i // per_seq, nb), 0, 0)),
                  pl.BlockSpec((1, d), lambda i: (0, 0)),
                  pl.BlockSpec((N_EXPERTS, d), lambda i: (0, 0)),
                  pl.BlockSpec((N_EXPERTS, 1), lambda i: (0, 0))],
        out_specs=pl.BlockSpec((None, 8, tm), lambda i: (i, 0, 0)),
        out_shape=jax.ShapeDtypeStruct((nt, 8, tm), F32),
        compiler_params=_params("arbitrary"),
        name="router",
    )(xs, mod3, nw, router_w.T, router_b.reshape(N_EXPERTS, 1))


def _load_tile_indices(dest_hbm, idx_smem, sem, n):
    cp = pltpu.make_async_copy(dest_hbm.at[pl.ds(pl.program_id(0) * n, n)], idx_smem, sem)
    cp.start()
    cp.wait()


def _dispatch_kernel(dest_hbm, x_ref, mod_ref, nw_ref, init_hbm, xs_hbm, v_ref, idx_smem, sem_idx, sem, *, tm):
    del init_hbm
    _load_tile_indices(dest_hbm, idx_smem, sem_idx, idx_smem.shape[0])
    v_ref[...] = _norm_mod(x_ref[...], nw_ref[...], mod_ref[...], 3, 4)

    def row_copy(t, d):
        return pltpu.make_async_copy(v_ref.at[pl.ds(t, 1)], xs_hbm.at[pl.ds(d, 1)], sem)

    def issue(t, c):
        row_copy(t, idx_smem[t]).start()
        return c

    lax.fori_loop(0, tm, issue, 0, unroll=8)
    pltpu.make_async_copy(v_ref, xs_hbm.at[pl.ds(0, tm)], sem).wait()


def _dispatch(dest_tiles, xs, mod3, nw, xs_init, *, tm, n_lat, seq):
    tt, d = xs.shape
    nb = mod3.shape[0] - 1
    per_seq = seq // tm
    n_idx = dest_tiles.shape[0] // (tt // tm)
    return pl.pallas_call(
        functools.partial(_dispatch_kernel, tm=tm),
        grid=(tt // tm,),
        in_specs=[pl.BlockSpec(memory_space=pl.ANY),
                  pl.BlockSpec((tm, d), lambda i: (i, 0)),
                  pl.BlockSpec((None, 6, d), lambda i: (jnp.minimum(i // per_seq, nb), 0, 0)),
                  pl.BlockSpec((1, d), lambda i: (0, 0)),
                  pl.BlockSpec(memory_space=pl.ANY)],
        out_specs=pl.BlockSpec(memory_space=pl.ANY),
        out_shape=jax.ShapeDtypeStruct(xs_init.shape, F32),
        scratch_shapes=[pltpu.VMEM((tm, d), F32), pltpu.SMEM((n_idx,), jnp.int32),
                        pltpu.SemaphoreType.DMA(()), pltpu.SemaphoreType.DMA(())],
        input_output_aliases={4: 0},
        compiler_params=_params("arbitrary"),
        name="moe_dispatch",
    )(dest_tiles, xs, mod3, nw, xs_init)


def _expert_kernel(wt_ref, we_ref, ws_ref, wv_ref, x_ref, wg_ref, wu_ref, wd_ref, o_ref):
    del wt_ref, we_ref, ws_ref
    n = pl.program_id(0)

    @pl.when(wv_ref[n] > 0)
    def _():
        x = x_ref[...].astype(BF16)
        hg = jnp.dot(x, wg_ref[...], preferred_element_type=F32)
        hu = jnp.dot(x, wu_ref[...], preferred_element_type=F32)
        act = (_silu(hg) * hu).astype(BF16)
        o_ref[...] = jnp.dot(act, wd_ref[...], preferred_element_type=F32)

    @pl.when(wv_ref[n] == 0)
    def _():
        o_ref[...] = jnp.zeros(o_ref.shape, F32)


def _experts(work_tile, work_expert, work_slot, work_valid, xs, wg, wu, wd, *, tm, layer):
    rows, d = xs.shape
    dff = wg.shape[3]
    grid_spec = pltpu.PrefetchScalarGridSpec(
        num_scalar_prefetch=4,
        grid=(work_tile.shape[0],),
        in_specs=[pl.BlockSpec((tm, d), lambda n, wt, we, ws, wv: (wt[n], 0)),
                  pl.BlockSpec((None, None, d, dff), lambda n, wt, we, ws, wv: (layer, we[n], 0, 0)),
                  pl.BlockSpec((None, None, d, dff), lambda n, wt, we, ws, wv: (layer, we[n], 0, 0)),
                  pl.BlockSpec((None, None, dff, d), lambda n, wt, we, ws, wv: (layer, we[n], 0, 0))],
        out_specs=pl.BlockSpec((tm, d), lambda n, wt, we, ws, wv: (wt[n], ws[n])),
    )
    return pl.pallas_call(
        _expert_kernel,
        grid_spec=grid_spec,
        out_shape=jax.ShapeDtypeStruct((rows, 2 * d), F32),
        compiler_params=_params("arbitrary"),
        name="moe_experts",
    )(work_tile, work_expert, work_slot, work_valid, xs, wg, wu, wd)


def _combine_kernel(dest_hbm, ys_hbm, x_ref, w_ref, mod_ref, *rest, tm):
    fnw_ref = rest[0] if len(rest) == 6 else None
    o_ref, idx_smem, y_ref, sem_idx, sem = rest[-5:]
    _load_tile_indices(dest_hbm, idx_smem, sem_idx, idx_smem.shape[0])
    d = x_ref.shape[1]

    def issue(t, c):
        pltpu.make_async_copy(ys_hbm.at[pl.ds(idx_smem[t], 1)], y_ref.at[pl.ds(t, 1)], sem).start()
        return c

    lax.fori_loop(0, tm, issue, 0, unroll=8)
    pltpu.make_async_copy(ys_hbm.at[pl.ds(0, tm)], y_ref, sem).wait()
    w = w_ref[...]
    ff = w[:, 0:1] * y_ref[:, :d] + w[:, 1:2] * y_ref[:, d:]
    out = x_ref[...] + mod_ref[5:6, :] * ff
    if fnw_ref is not None:
        out = out * lax.rsqrt(jnp.mean(out * out, axis=-1, keepdims=True) + EPS) * fnw_ref[...]
    o_ref[...] = out


def _combine(dest_tiles, ys, xs, w12, mod3, final_nw, *, tm, n_lat, seq):
    tt, d = xs.shape
    nb = mod3.shape[0] - 1
    per_seq = seq // tm
    n_idx = dest_tiles.shape[0] // (tt // tm)
    in_specs = [pl.BlockSpec(memory_space=pl.ANY),
                pl.BlockSpec(memory_space=pl.ANY),
                pl.BlockSpec((tm, d), lambda i: (i, 0)),
                pl.BlockSpec((tm, 2), lambda i: (i, 0)),
                pl.BlockSpec((None, 6, d), lambda i: (jnp.minimum(i // per_seq, nb), 0, 0))]
    args = [dest_tiles, ys, xs, w12, mod3]
    if final_nw is not None:
        in_specs.append(pl.BlockSpec((1, d), lambda i: (0, 0)))
        args.append(final_nw.reshape(1, d))
    return pl.pallas_call(
        functools.partial(_combine_kernel, tm=tm),
        grid=(tt // tm,),
        in_specs=in_specs,
        out_specs=pl.BlockSpec((tm, d), lambda i: (i, 0)),
        out_shape=jax.ShapeDtypeStruct((tt, d), F32),
        scratch_shapes=[pltpu.SMEM((n_idx,), jnp.int32), pltpu.VMEM((tm, 2 * d), F32),
                        pltpu.SemaphoreType.DMA(()), pltpu.SemaphoreType.DMA(())],
        compiler_params=_params("arbitrary"),
        name="moe_combine",
    )(*args)


PAIRS_PER_GROUP = EXPERTS_PER_GROUP * (EXPERTS_PER_GROUP - 1) // 2
N_CLASSES = N_GROUPS * PAIRS_PER_GROUP
IDX_ALIGN = 1024


def _moe_sorted_rows(tt):
    tme = _tile(256, tt)
    return tme, tt // tme + N_CLASSES


def _moe(xs, xs_sorted_init, mod3, nw, router_w, router_b, wg, wu, wd, final_nw, *, n_lat, seq, layer):
    tt, d = xs.shape
    route = _router(xs, mod3, nw, router_w, router_b, n_lat=n_lat, seq=seq)
    route = route.transpose(1, 0, 2).reshape(8, tt)
    e_lo, e_hi = route[0].astype(jnp.int32), route[1].astype(jnp.int32)
    w12 = route[2:4].T

    tme, n_tiles = _moe_sorted_rows(tt)
    epg = EXPERTS_PER_GROUP
    lo, hi = e_lo % epg, e_hi % epg
    cls = (e_lo // epg) * PAIRS_PER_GROUP + (lo * (2 * epg - 1 - lo)) // 2 + (hi - lo - 1)
    onehot = (cls[:, None] == jnp.arange(N_CLASSES, dtype=jnp.int32)[None, :]).astype(jnp.int32)
    csum = jnp.cumsum(onehot, axis=0)
    rank = jnp.sum(onehot * csum, axis=1) - 1
    cls_tiles = (csum[-1] + tme - 1) // tme
    tile_end = jnp.cumsum(cls_tiles)
    tile_start = tile_end - cls_tiles
    dest = jnp.sum(onehot * (tile_start * tme)[None, :], axis=1) + rank

    pairs = [(a, b) for a in range(epg) for b in range(a + 1, epg)]
    segs = sorted((g * epg + pair[side], g * PAIRS_PER_GROUP + pi, side)
                  for g in range(N_GROUPS) for pi, pair in enumerate(pairs) for side in (0, 1))
    seg_expert = jnp.array([s[0] for s in segs], jnp.int32)
    seg_cls = jnp.array([s[1] for s in segs], jnp.int32)
    seg_side = jnp.array([s[2] for s in segs], jnp.int32)
    seg_len = cls_tiles[seg_cls]
    seg_end = jnp.cumsum(seg_len)
    seg_start = seg_end - seg_len
    w_idx = jnp.arange(2 * n_tiles, dtype=jnp.int32)
    w_seg = jnp.minimum(jnp.sum((w_idx[:, None] >= seg_end[None, :]).astype(jnp.int32), axis=1), len(segs) - 1)
    work_valid = (w_idx < seg_end[-1]).astype(jnp.int32)
    work_tile = jnp.where(work_valid > 0, tile_start[seg_cls[w_seg]] + w_idx - seg_start[w_seg], n_tiles - 1)
    work_slot = seg_side[w_seg] * work_valid
    work_expert = seg_expert[w_seg]

    tmd = _row_tile(512, seq, tt, n_lat)
    n_idx = IDX_ALIGN * ((tmd + IDX_ALIGN - 1) // IDX_ALIGN)
    dest_tiles = jnp.pad(dest.reshape(tt // tmd, tmd), ((0, 0), (0, n_idx - tmd))).reshape(-1)
    xs_sorted = _dispatch(dest_tiles, xs, mod3, nw, xs_sorted_init, tm=tmd, n_lat=n_lat, seq=seq)
    ys = _experts(work_tile, work_expert, work_slot, work_valid, xs_sorted, wg, wu, wd, tm=tme, layer=layer)
    return _combine(dest_tiles, ys, xs, w12, mod3, final_nw, tm=tmd, n_lat=n_lat, seq=seq), xs_sorted


def _rope_tables(seq):
    rows = seq // GRID_W
    r = jnp.repeat(jnp.arange(rows, dtype=F32), GRID_W)
    col = jnp.tile(jnp.arange(GRID_W, dtype=F32), rows)
    n_freq = HEAD_DIM // 4
    inv = ROPE_BASE ** (-jnp.arange(n_freq, dtype=F32) / n_freq)
    ar, ac = r[:, None] * inv, col[:, None] * inv
    cos_t = jnp.concatenate([jnp.cos(ar), jnp.cos(ar), jnp.cos(ac), jnp.cos(ac)], axis=1)
    sin_t = jnp.concatenate([-jnp.sin(ar), jnp.sin(ar), -jnp.sin(ac), jnp.sin(ac)], axis=1)
    return cos_t, sin_t


def _even_layer(xs, mod3, nw, w_in, w_out, sink, rope, fconst_lat, fconst_ctx, *, nbatch, seq, n_ctx, n_out_rows):
    n_lat = nbatch * seq
    cos_t, sin_t = rope
    p = _norm_mm(xs, mod3, nw, w_in.astype(BF16), cos_t, sin_t, n_lat=n_lat, seq=seq, rope_cols=A_Q_W + A_KV_W)
    a = _win_attn(p, sink, nbatch=nbatch, seq=seq, n_ctx=n_ctx)
    a = _ctx_attn(p, sink, a, nbatch=nbatch, n_lat=n_lat, n_ctx=n_ctx)
    f = _fourier(p, *fconst_lat, None, nbatch=nbatch, t=seq, row_blk0=0)
    f = _fourier(p, *fconst_ctx, f, nbatch=nbatch, t=n_ctx, row_blk0=n_lat // n_ctx)
    wo = w_out.astype(BF16)
    return _res_mm([a, f], [wo[:A_Q_W], wo[A_Q_W:]], xs, mod3, gate_row=2, n_lat=n_lat, seq=seq,
                   n_out_rows=n_out_rows)


def _odd_layer(xs, mod3, nw, w_in, conv_w, a_log, dt_bias, norm_w, w_out, rope, *, nbatch, seq, n_ctx, n_out_rows):
    n_lat = nbatch * seq
    d = xs.shape[1]
    cos_t, sin_t = rope
    n_main = 4 * C_W
    w_main = w_in[:, :n_main].astype(BF16)
    w_gate = jnp.concatenate([w_in[:, n_main:], jnp.zeros((d, GATE_W - 4 * C_HEADS), F32)], axis=1).astype(BF16)
    p, gates = _norm_mm(xs, mod3, nw, w_main, cos_t, sin_t, n_lat=n_lat, seq=seq, rope_cols=0, w_side=w_gate)
    qkv = _conv_prep(p, conv_w, None, nseq=nbatch, t=seq, row_blk0=0)
    qkv = _conv_prep(p, conv_w, qkv, nseq=nbatch, t=n_ctx, row_blk0=n_lat // n_ctx)
    o_f, o_b = _delta_scan(qkv, gates, a_log, dt_bias, nbatch=nbatch, seq=seq, n_ctx=n_ctx)
    on = _gated_norm(o_f, o_b, p, norm_w)
    return _res_mm([on], [w_out.astype(BF16)], xs, mod3, gate_row=2, n_lat=n_lat, seq=seq, n_out_rows=n_out_rows)


def kernel(x, c, ctx, c_ctx, adaln_w, adaln_b, norm_mix_w, norm_ffn_w, attn_in_w, attn_out_w, attn_sink, dn_in_w, dn_conv_w, dn_a_log, dn_dt_bias, dn_norm_w, dn_out_w, router_w, router_b, exp_gate_w, exp_up_w, exp_down_w, final_norm_w):
    nbatch, seq, d = x.shape
    n_ctx = ctx.shape[1]
    depth = adaln_w.shape[0]
    n_lat = nbatch * seq
    assert seq % BLOCK == 0 and seq % GRID_W == 0 and n_ctx % C_CHUNK == 0 and n_lat % n_ctx == 0

    xs = jnp.concatenate([x.reshape(n_lat, d), ctx.reshape(nbatch * n_ctx, d)], axis=0)
    mod_rows = 8 * ((nbatch + 1 + 7) // 8)
    c_all = jnp.concatenate([c, c_ctx[None, :], jnp.zeros((mod_rows - nbatch - 1, d), F32)], axis=0)
    mod_all = _adaln(c_all, adaln_w, adaln_b)[:, :nbatch + 1].reshape(depth, nbatch + 1, 6, d)

    rope = _rope_tables(seq)
    fconst_lat = _fourier_consts(seq)
    fconst_ctx = _fourier_consts(n_ctx)
    tme, n_tiles = _moe_sorted_rows(xs.shape[0])
    xs_sorted = jnp.zeros((n_tiles * tme, d), F32)
    wg_all, wu_all, wd_all = exp_gate_w.astype(BF16), exp_up_w.astype(BF16), exp_down_w.astype(BF16)

    for layer in range(depth):
        mod3 = mod_all[layer]
        i = layer // 2
        nw = norm_mix_w[layer].reshape(1, d)
        last = layer == depth - 1
        n_out_rows = n_lat if last else xs.shape[0]
        if layer % 2 == 0:
            xs = _even_layer(xs, mod3, nw, attn_in_w[i], attn_out_w[i], attn_sink[i], rope, fconst_lat, fconst_ctx,
                             nbatch=nbatch, seq=seq, n_ctx=n_ctx, n_out_rows=n_out_rows)
        else:
            xs = _odd_layer(xs, mod3, nw, dn_in_w[i], dn_conv_w[i], dn_a_log[i], dn_dt_bias[i], dn_norm_w[i],
                            dn_out_w[i], rope, nbatch=nbatch, seq=seq, n_ctx=n_ctx, n_out_rows=n_out_rows)
        xs, xs_sorted = _moe(xs, xs_sorted, mod3, norm_ffn_w[layer].reshape(1, d), router_w, router_b,
                             wg_all, wu_all, wd_all, final_norm_w if last else None,
                             n_lat=n_lat, seq=seq, layer=layer)
    return xs.reshape(nbatch, seq, d)
```

```python
import functools
import math

import jax
import jax.numpy as jnp
from jax import lax
from jax.experimental import pallas as pl
from jax.experimental.pallas import tpu as pltpu

F32 = jnp.float32
BF16 = jnp.bfloat16
EPS = 1e-6

HEAD_DIM = 128
A_Q_HEADS = 12
A_KV_HEADS = 4
A_GROUP = A_Q_HEADS // A_KV_HEADS
BLOCK = 128
GRID_W = 64
ROPE_BASE = 10000.0
B_GROUPS = 4
B_GROUP_DIM = 128
C_HEADS = 16
C_DK = 128
C_CONV = 5
C_CHUNK = 64
N_EXPERTS = 16
N_GROUPS = 4
EXPERTS_PER_GROUP = N_EXPERTS // N_GROUPS

A_Q_W = A_Q_HEADS * HEAD_DIM
A_KV_W = A_KV_HEADS * HEAD_DIM
B_W = B_GROUPS * B_GROUP_DIM
C_W = C_HEADS * C_DK
GATE_W = 128

V7X_VMEM_LIMIT_BYTES = 56 * 1024 * 1024
NEG_BIG = -1e30


def _tile(pref, *dims, mult=8):
    t = min((pref,) + dims)
    t -= t % mult
    while t > mult and any(d % t for d in dims):
        t -= mult
    assert t >= mult and all(d % t == 0 for d in dims), (pref, dims)
    return t


def _row_tile(pref, seq, tt, n_lat):
    return _tile(pref, seq, n_lat, *((tt - n_lat,) if tt > n_lat else ()))


def _params(*sem):
    return pltpu.CompilerParams(dimension_semantics=sem, vmem_limit_bytes=V7X_VMEM_LIMIT_BYTES)


def _bdot(a, b):
    return jnp.dot(a.astype(BF16), b.astype(BF16), preferred_element_type=F32)


def _bdot_nt(a, b):
    return lax.dot_general(a.astype(BF16), b.astype(BF16), (((1,), (1,)), ((), ())), preferred_element_type=F32)


def _bdot_tn(a, b):
    return lax.dot_general(a.astype(BF16), b.astype(BF16), (((0,), (0,)), ((), ())), preferred_element_type=F32)


def _split(a):
    hi = a.astype(BF16)
    lo = (a - hi.astype(F32)).astype(BF16)
    return hi, lo


def _sigmoid(x):
    return 1.0 / (1.0 + jnp.exp(-x))


def _silu(x):
    return x * _sigmoid(x)


def _adaln_kernel(c_ref, w_ref, b_ref, o_ref):
    sc = _silu(c_ref[...])
    sh, sl = _split(sc)
    wh, wl = _split(w_ref[...])
    acc = (jnp.dot(sh, wh, preferred_element_type=F32) + jnp.dot(sh, wl, preferred_element_type=F32)
           + jnp.dot(sl, wh, preferred_element_type=F32))
    o_ref[...] = acc + b_ref[...]


def _adaln(c_all, adaln_w, adaln_b):
    depth, d, n6 = adaln_w.shape
    rows = c_all.shape[0]
    tn = _tile(512, n6, mult=128)
    return pl.pallas_call(
        _adaln_kernel,
        grid=(depth, n6 // tn),
        in_specs=[pl.BlockSpec((rows, d), lambda l, j: (0, 0)),
                  pl.BlockSpec((None, d, tn), lambda l, j: (l, 0, j)),
                  pl.BlockSpec((None, 1, tn), lambda l, j: (l, 0, j))],
        out_specs=pl.BlockSpec((None, rows, tn), lambda l, j: (l, 0, j)),
        out_shape=jax.ShapeDtypeStruct((depth, rows, n6), F32),
        compiler_params=_params("arbitrary", "arbitrary"),
        name="adaln",
    )(c_all, adaln_w, adaln_b.reshape(depth, 1, n6))


def _norm_mod(x, nw, mod, shift_row, scale_row):
    ms = jnp.mean(x * x, axis=-1, keepdims=True)
    y = x * lax.rsqrt(ms + EPS) * nw
    return y * (1.0 + mod[scale_row:scale_row + 1, :]) + mod[shift_row:shift_row + 1, :]


def _rope_slab(a, cos, sin):
    lane = lax.broadcasted_iota(jnp.int32, a.shape, 1)
    first = (lane % 64) < 32
    partner = jnp.where(first, pltpu.roll(a, 96, 1), pltpu.roll(a, 32, 1))
    return a * cos + partner * sin


def _norm_mm_kernel(x_ref, mod_ref, nw_ref, w_ref, cos_ref, sin_ref, *rest, rope_tiles, n_lat_tiles, tn):
    if len(rest) == 4:
        w2_ref, o_ref, o2_ref, u_ref = rest
    else:
        (o_ref, u_ref), w2_ref, o2_ref = rest, None, None
    i = pl.program_id(0)
    j = pl.program_id(1)

    @pl.when(j == 0)
    def _():
        u_ref[...] = _norm_mod(x_ref[...], nw_ref[...], mod_ref[...], 0, 1).astype(BF16)
        if w2_ref is not None:
            o2_ref[...] = jnp.dot(u_ref[...], w2_ref[...], preferred_element_type=F32)

    acc = jnp.dot(u_ref[...], w_ref[...], preferred_element_type=F32)
    if rope_tiles == 0:
        o_ref[...] = acc.astype(o_ref.dtype)
    else:
        do_rope = jnp.logical_and(j < rope_tiles, i < n_lat_tiles)

        @pl.when(do_rope)
        def _():
            cos = cos_ref[...]
            sin = sin_ref[...]
            for s in range(tn // HEAD_DIM):
                sl = slice(s * HEAD_DIM, (s + 1) * HEAD_DIM)
                o_ref[:, sl] = _rope_slab(acc[:, sl], cos, sin).astype(o_ref.dtype)

        @pl.when(jnp.logical_not(do_rope))
        def _():
            o_ref[...] = acc.astype(o_ref.dtype)


def _norm_mm(xs, mod3, nw, w, cos_t, sin_t, *, n_lat, seq, rope_cols, w_side=None):
    tt, d = xs.shape
    nout = w.shape[1]
    nb = mod3.shape[0] - 1
    tm = _row_tile(1024, seq, tt, n_lat)
    tn = _tile(1024, nout, *((rope_cols,) if rope_cols else ()), mult=128)
    per_seq = seq // tm
    kern = functools.partial(_norm_mm_kernel, rope_tiles=rope_cols // tn, n_lat_tiles=n_lat // tm, tn=tn)
    in_specs = [pl.BlockSpec((tm, d), lambda i, j: (i, 0)),
                pl.BlockSpec((None, 6, d), lambda i, j: (jnp.minimum(i // per_seq, nb), 0, 0)),
                pl.BlockSpec((1, d), lambda i, j: (0, 0)),
                pl.BlockSpec((d, tn), lambda i, j: (0, j)),
                pl.BlockSpec((tm, HEAD_DIM), lambda i, j: (i % per_seq, 0)),
                pl.BlockSpec((tm, HEAD_DIM), lambda i, j: (i % per_seq, 0))]
    out_specs = [pl.BlockSpec((tm, tn), lambda i, j: (i, j))]
    out_shape = [jax.ShapeDtypeStruct((tt, nout), BF16)]
    args = [xs, mod3, nw, w, cos_t, sin_t]
    if w_side is not None:
        nside = w_side.shape[1]
        in_specs.append(pl.BlockSpec((d, nside), lambda i, j: (0, 0)))
        out_specs.append(pl.BlockSpec((tm, nside), lambda i, j: (i, 0)))
        out_shape.append(jax.ShapeDtypeStruct((tt, nside), F32))
        args.append(w_side)
    outs = pl.pallas_call(
        kern,
        grid=(tt // tm, nout // tn),
        in_specs=in_specs,
        out_specs=out_specs,
        out_shape=out_shape,
        scratch_shapes=[pltpu.VMEM((tm, d), BF16)],
        compiler_params=_params("arbitrary", "arbitrary"),
        name="norm_mm",
    )(*args)
    return outs if w_side is not None else outs[0]


def _res_mm_kernel(*refs, n_parts, gate_row):
    a_refs = refs[:n_parts]
    w_refs = refs[n_parts:2 * n_parts]
    x_ref, mod_ref, o_ref = refs[2 * n_parts:]
    acc = jnp.dot(a_refs[0][...], w_refs[0][...], preferred_element_type=F32)
    for a_ref, w_ref in zip(a_refs[1:], w_refs[1:]):
        acc = acc + jnp.dot(a_ref[...], w_ref[...], preferred_element_type=F32)
    o_ref[...] = x_ref[...] + mod_ref[gate_row:gate_row + 1, :] * acc


def _res_mm(parts, weights, xs, mod3, *, gate_row, n_lat, seq, n_out_rows):
    tt, d = n_out_rows, xs.shape[1]
    nb = mod3.shape[0] - 1
    tm = _row_tile(512, seq, tt, n_lat)
    tn = d
    per_seq = seq // tm
    n_parts = len(parts)
    in_specs = ([pl.BlockSpec((tm, a.shape[1]), lambda i, j: (i, 0)) for a in parts]
                + [pl.BlockSpec((w.shape[0], tn), lambda i, j: (0, j)) for w in weights]
                + [pl.BlockSpec((tm, tn), lambda i, j: (i, j)),
                   pl.BlockSpec((None, 6, tn), lambda i, j: (jnp.minimum(i // per_seq, nb), 0, j))])
    return pl.pallas_call(
        functools.partial(_res_mm_kernel, n_parts=n_parts, gate_row=gate_row),
        grid=(tt // tm, d // tn),
        in_specs=in_specs,
        out_specs=pl.BlockSpec((tm, tn), lambda i, j: (i, j)),
        out_shape=jax.ShapeDtypeStruct((tt, d), F32),
        compiler_params=_params("arbitrary", "arbitrary"),
        name="res_mm",
    )(*parts, *weights, xs, mod3)


def _softmax_sink_pv(s, sk, v):
    m = jnp.maximum(jnp.max(s, axis=-1, keepdims=True), sk)
    p = jnp.exp(s - m)
    den = jnp.sum(p, axis=-1, keepdims=True) + jnp.exp(sk - m)
    return _bdot(p, v) / den


def _sink_column(sink_ref, h, rows_per_head):
    r = lax.broadcasted_iota(jnp.int32, (A_GROUP * rows_per_head, 1), 0) // rows_per_head
    sk = jnp.zeros((A_GROUP * rows_per_head, 1), F32)
    for g in range(A_GROUP):
        sk = jnp.where(r == g, sink_ref[h * A_GROUP + g], sk)
    return sk


def _stack_heads(q):
    return jnp.concatenate([q[:, g * HEAD_DIM:(g + 1) * HEAD_DIM] for g in range(A_GROUP)], axis=0)


def _unstack_heads(o, rows):
    return jnp.concatenate([o[g * rows:(g + 1) * rows, :] for g in range(A_GROUP)], axis=1)


def _win_attn_kernel(sink_ref, q_ref, km_ref, k0_ref, kp_ref, kc_ref, vm_ref, v0_ref, vp_ref, vc_ref, o_ref, *, nblk):
    i = pl.program_id(1)
    n_ctx = kc_ref.shape[0]
    nk = 3 * BLOCK + n_ctx
    r = lax.broadcasted_iota(jnp.int32, (A_GROUP * BLOCK, nk), 0) % BLOCK
    c = lax.broadcasted_iota(jnp.int32, (A_GROUP * BLOCK, nk), 1)
    band = jnp.logical_and(c >= r, c <= r + 2 * BLOCK)
    lo_ok = jnp.logical_or(c >= BLOCK, i > 0)
    hi_ok = jnp.logical_or(c < 2 * BLOCK, i < nblk - 1)
    valid = jnp.logical_or(c >= 3 * BLOCK, jnp.logical_and(band, jnp.logical_and(lo_ok, hi_ok)))
    for h in range(A_KV_HEADS):
        hs = slice(h * HEAD_DIM, (h + 1) * HEAD_DIM)
        qsl = slice(h * A_GROUP * HEAD_DIM, (h + 1) * A_GROUP * HEAD_DIM)
        qs = _stack_heads(q_ref[:, qsl])
        k = jnp.concatenate([km_ref[:, hs], k0_ref[:, hs], kp_ref[:, hs], kc_ref[:, hs]], axis=0)
        v = jnp.concatenate([vm_ref[:, hs], v0_ref[:, hs], vp_ref[:, hs], vc_ref[:, hs]], axis=0)
        s = jnp.where(valid, _bdot_nt(qs, k) * (HEAD_DIM ** -0.5), NEG_BIG)
        o = _softmax_sink_pv(s, _sink_column(sink_ref, h, BLOCK), v)
        o_ref[:, qsl] = _unstack_heads(o, BLOCK).astype(o_ref.dtype)


def _win_attn(p, sink, *, nbatch, seq, n_ctx):
    tt = p.shape[0]
    nblk = seq // BLOCK
    n_lat = nbatch * seq
    ctx_blk0 = n_lat // n_ctx
    kcol = A_Q_W // A_KV_W
    vcol = (A_Q_W + A_KV_W) // A_KV_W

    def kv_spec(col, off):
        return pl.BlockSpec((BLOCK, A_KV_W), lambda b, i, s: (b * nblk + jnp.clip(i + off, 0, nblk - 1), col))

    def ctx_spec(col):
        return pl.BlockSpec((n_ctx, A_KV_W), lambda b, i, s: (ctx_blk0 + b, col))

    grid_spec = pltpu.PrefetchScalarGridSpec(
        num_scalar_prefetch=1,
        grid=(nbatch, nblk),
        in_specs=[pl.BlockSpec((BLOCK, A_Q_W), lambda b, i, s: (b * nblk + i, 0)),
                  kv_spec(kcol, -1), kv_spec(kcol, 0), kv_spec(kcol, 1), ctx_spec(kcol),
                  kv_spec(vcol, -1), kv_spec(vcol, 0), kv_spec(vcol, 1), ctx_spec(vcol)],
        out_specs=pl.BlockSpec((BLOCK, A_Q_W), lambda b, i, s: (b * nblk + i, 0)),
    )
    return pl.pallas_call(
        functools.partial(_win_attn_kernel, nblk=nblk),
        grid_spec=grid_spec,
        out_shape=jax.ShapeDtypeStruct((tt, A_Q_W), BF16),
        compiler_params=_params("arbitrary", "arbitrary"),
        name="win_attn",
    )(sink, p, p, p, p, p, p, p, p, p)


def _ctx_attn_kernel(sink_ref, q_ref, k_ref, v_ref, prev_ref, o_ref):
    del prev_ref
    h = pl.program_id(1)
    n_ctx = q_ref.shape[0]
    s = _bdot_nt(_stack_heads(q_ref[...]), k_ref[...]) * (HEAD_DIM ** -0.5)
    o = _softmax_sink_pv(s, _sink_column(sink_ref, h, n_ctx), v_ref[...])
    o_ref[...] = _unstack_heads(o, n_ctx).astype(o_ref.dtype)


def _ctx_attn(p, sink, a_lat, *, nbatch, n_lat, n_ctx):
    ctx_blk0 = n_lat // n_ctx
    kcol = A_Q_W // HEAD_DIM
    vcol = (A_Q_W + A_KV_W) // HEAD_DIM
    grid_spec = pltpu.PrefetchScalarGridSpec(
        num_scalar_prefetch=1,
        grid=(nbatch, A_KV_HEADS),
        in_specs=[pl.BlockSpec((n_ctx, A_GROUP * HEAD_DIM), lambda b, h, s: (ctx_blk0 + b, h)),
                  pl.BlockSpec((n_ctx, HEAD_DIM), lambda b, h, s: (ctx_blk0 + b, kcol + h)),
                  pl.BlockSpec((n_ctx, HEAD_DIM), lambda b, h, s: (ctx_blk0 + b, vcol + h)),
                  pl.BlockSpec(memory_space=pl.ANY)],
        out_specs=pl.BlockSpec((n_ctx, A_GROUP * HEAD_DIM), lambda b, h, s: (ctx_blk0 + b, h)),
    )
    return pl.pallas_call(
        _ctx_attn_kernel,
        grid_spec=grid_spec,
        out_shape=jax.ShapeDtypeStruct(a_lat.shape, BF16),
        input_output_aliases={4: 0},
        compiler_params=_params("arbitrary", "arbitrary"),
        name="ctx_attn",
    )(sink, p, p, p, a_lat)


def _dft_tables(t):
    k = jnp.arange(t, dtype=jnp.int32)
    ang = ((k[:, None] * k[None, :]) % t).astype(F32) * (2.0 * math.pi / t)
    scale = t ** -0.5
    return jnp.cos(ang) * scale, jnp.sin(ang) * scale


def _fourier_kernel(f_ref, w1_ref, dft_ref, *rest):
    o_ref, stk_ref = rest[-2:]
    t = f_ref.shape[0]

    @pl.when(pl.program_id(1) == 0)
    def _():
        f1 = jnp.dot(f_ref[...], w1_ref[...], preferred_element_type=F32)
        stk_ref[0:t, :] = f1[:, :B_W].astype(BF16)
        stk_ref[t:2 * t, :] = f1[:, B_W:].astype(BF16)

    o_ref[...] = jnp.dot(dft_ref[...], stk_ref[...], preferred_element_type=F32).astype(o_ref.dtype)


def _fourier(p, w1, dft2, prev, *, nbatch, t, row_blk0):
    fcol = (A_Q_W + 2 * A_KV_W) // B_W
    tq = _tile(512, t)
    nq = t // tq
    in_specs = [pl.BlockSpec((t, B_W), lambda b, q: (row_blk0 + b, fcol)),
                pl.BlockSpec((B_W, 2 * B_W), lambda b, q: (0, 0)),
                pl.BlockSpec((tq, 2 * t), lambda b, q: (q, 0))]
    args = [p, w1, dft2]
    aliases = {}
    if prev is not None:
        in_specs.append(pl.BlockSpec(memory_space=pl.ANY))
        args.append(prev)
        aliases = {3: 0}
    return pl.pallas_call(
        _fourier_kernel,
        grid=(nbatch, nq),
        in_specs=in_specs,
        out_specs=pl.BlockSpec((tq, B_W), lambda b, q: ((row_blk0 + b) * nq + q, 0)),
        out_shape=jax.ShapeDtypeStruct((p.shape[0], B_W), BF16),
        scratch_shapes=[pltpu.VMEM((2 * t, B_W), BF16)],
        input_output_aliases=aliases,
        compiler_params=_params("arbitrary", "arbitrary"),
        name="fourier",
    )(*args)


def _fourier_consts(t):
    cd, sd = _dft_tables(B_GROUP_DIM)
    eye = jnp.eye(B_GROUPS, dtype=F32)
    w1 = jnp.concatenate([jnp.kron(eye, cd), jnp.kron(eye, sd)], axis=1).astype(BF16)
    ct, st = _dft_tables(t)
    dft2 = jnp.concatenate([ct, -st], axis=1).astype(BF16)
    return w1, dft2


def _conv_prep_kernel(p_ref, w_ref, *rest, tc):
    o_ref, pad_ref = rest[-2:]
    j = pl.program_id(1)
    t = p_ref.shape[0]
    half = C_CONV // 2
    pad_ref[0:8, :] = jnp.zeros((8, tc), F32)
    pad_ref[t + 8:t + 16, :] = jnp.zeros((8, tc), F32)
    pad_ref[8:t + 8, :] = p_ref[...].astype(F32)
    w = w_ref[...]
    is_q = j < (C_W // tc)
    is_qk = j < (2 * C_W // tc)
    post = jnp.where(is_q, C_DK ** -0.5, 1.0)
    rc = _tile(256, t)
    for r0 in range(0, t, rc):
        win = pad_ref[r0:r0 + rc + 16, :]
        acc = w[half:half + 1, :] * win[8:8 + rc, :]
        for jj in range(C_CONV):
            if jj != half:
                shifted = pltpu.roll(win, (half - jj) % (rc + 16), 0)
                acc = acc + w[jj:jj + 1, :] * shifted[8:8 + rc, :]
        y = _silu(acc)
        for hh in range(tc // C_DK):
            sl = slice(hh * C_DK, (hh + 1) * C_DK)
            yh = y[:, sl]
            nrm = lax.rsqrt(jnp.sum(yh * yh, axis=-1, keepdims=True) + EPS) * post
            o_ref[r0:r0 + rc, sl] = (yh * jnp.where(is_qk, nrm, 1.0)).astype(o_ref.dtype)


def _conv_prep(p, conv_w, prev, *, nseq, t, row_blk0):
    tc = 512
    ncol = 3 * C_W // tc
    in_specs = [pl.BlockSpec((t, tc), lambda b, j: (row_blk0 + b, j)),
                pl.BlockSpec((C_CONV, tc), lambda b, j: (0, j))]
    args = [p, conv_w]
    aliases = {}
    if prev is not None:
        in_specs.append(pl.BlockSpec(memory_space=pl.ANY))
        args.append(prev)
        aliases = {2: 0}
    return pl.pallas_call(
        functools.partial(_conv_prep_kernel, tc=tc),
        grid=(nseq, ncol),
        in_specs=in_specs,
        out_specs=pl.BlockSpec((t, tc), lambda b, j: (row_blk0 + b, j)),
        out_shape=jax.ShapeDtypeStruct((p.shape[0], 3 * C_W), BF16),
        scratch_shapes=[pltpu.VMEM((t + 16, tc), F32)],
        input_output_aliases=aliases,
        compiler_params=_params("arbitrary", "arbitrary"),
        name="conv_prep",
    )(*args)


HEADS_PER_PACK = 2
PACK_W = HEADS_PER_PACK * C_CHUNK
PACK_C = HEADS_PER_PACK * C_DK


def _split3(a):
    hi = a.astype(BF16)
    r1 = a - hi.astype(F32)
    mid = r1.astype(BF16)
    lo = (r1 - mid.astype(F32)).astype(BF16)
    return hi, mid, lo


def _dot3(m, a):
    hi, mid, lo = _split3(a)
    return (jnp.dot(m, hi, preferred_element_type=F32) + jnp.dot(m, mid, preferred_element_type=F32)
            + jnp.dot(m, lo, preferred_element_type=F32))


def _block_diag(x, mask):
    xb = x.astype(BF16)
    return jnp.where(mask, jnp.concatenate([xb] * HEADS_PER_PACK, axis=0), jnp.zeros((), BF16))


def _delta_kernel(*refs):
    ndir = 2
    in_refs = [refs[4 * s:4 * s + 4] for s in range(ndir)]
    alog_ref, dtb_ref = refs[4 * ndir:4 * ndir + 2]
    o_refs = refs[4 * ndir + 2:5 * ndir + 2]
    s_ref = refs[5 * ndir + 2]
    step = pl.program_id(1)
    cc = C_CHUNK
    npack = C_HEADS // HEADS_PER_PACK

    @pl.when(step == 0)
    def _():
        s_ref[...] = jnp.zeros(s_ref.shape, F32)

    ri = lax.broadcasted_iota(jnp.int32, (cc, cc), 0)
    ci = lax.broadcasted_iota(jnp.int32, (cc, cc), 1)
    rw = lax.broadcasted_iota(jnp.int32, (cc, PACK_W), 0)
    lw = lax.broadcasted_iota(jnp.int32, (cc, PACK_W), 1)
    cw = lw % cc
    hw = lw // cc
    dir_masks = [(ci <= ri, cw <= rw, cw < rw, cc - 1), (ci >= ri, cw >= rw, cw > rw, 0)]
    eye_w = cw == rw
    bd_mask_w = (lax.broadcasted_iota(jnp.int32, (PACK_W, PACK_W), 0) // cc
                 == lax.broadcasted_iota(jnp.int32, (PACK_W, PACK_W), 1) // cc)
    bd_mask_c = (lax.broadcasted_iota(jnp.int32, (PACK_W, PACK_C), 0) // cc
                 == lax.broadcasted_iota(jnp.int32, (PACK_W, PACK_C), 1) // C_DK)

    def lanes(arr, cols, width):
        return jnp.concatenate([jnp.broadcast_to(arr[:, c:c + 1], (cc, width)) for c in cols], axis=1)

    packs = []
    for si, p in [(si, p) for si in range(ndir) for p in range(npack)]:
        q_ref, k_ref, v_ref, gt_ref = in_refs[si]
        d_idx = si
        if p == 0:
            incl_sq, incl, strict, last = dir_masks[si]
            tri = jnp.where(incl_sq, 1.0, 0.0).astype(BF16)
            gt = gt_ref[...]
            x = gt + dtb_ref[...]
            softplus = jnp.maximum(x, 0.0) + jnp.log(1.0 + jnp.exp(-jnp.abs(x)))
            g_all = -jnp.exp(alog_ref[...]) * softplus
            beta_all = _sigmoid(gt)
            gc_all = _dot3(tri, g_all)
            g_last = gc_all[last:last + 1, :]
            gc_t = jnp.concatenate([gc_all] * HEADS_PER_PACK, axis=0).T
            egc_all = jnp.exp(gc_all)
            edl_all = jnp.exp(g_last - gc_all)
            eg_last = jnp.exp(g_last)
        cols = [d_idx * C_HEADS + p * HEADS_PER_PACK + j for j in range(HEADS_PER_PACK)]
        sl = slice(p * PACK_C, (p + 1) * PACK_C)
        qn, kn, vn = q_ref[:, sl], k_ref[:, sl], v_ref[:, sl]
        kf = kn.astype(F32)
        beta_b = lanes(beta_all, [c + 2 * C_HEADS for c in cols], C_DK)
        egc_b = lanes(egc_all, cols, C_DK)
        kb = kf * beta_b
        vb = vn.astype(F32) * beta_b
        kbg = kb * egc_b
        qg = qn.astype(F32) * egc_b
        kdec = kf * lanes(edl_all, cols, C_DK)
        kkqk = _bdot_nt(jnp.concatenate([kb.astype(BF16), qn], axis=0), _block_diag(kn, bd_mask_c))
        gc_w = jnp.broadcast_to(gc_all[:, cols[-1]:cols[-1] + 1], (cc, PACK_W))
        for j in range(HEADS_PER_PACK - 1):
            gc_w = jnp.where(hw == j, gc_all[:, cols[j]:cols[j] + 1], gc_w)
        gr_w = jnp.broadcast_to(gc_t[cols[-1]:cols[-1] + 1, :], (cc, PACK_W))
        for j in range(HEADS_PER_PACK - 1):
            gr_w = jnp.where(hw == j, gc_t[cols[j]:cols[j] + 1, :], gr_w)
        decay = jnp.where(incl, jnp.exp(jnp.where(incl, gc_w - gr_w, 0.0)), 0.0)
        nmat = jnp.where(strict, kkqk[:cc] * decay, 0.0)
        a_qk = jnp.where(incl, kkqk[cc:] * decay, 0.0)
        packs.append(dict(cols=cols, sl=sl, vb=vb, kbg=kbg, qg=qg, kdec=kdec, a_qk=a_qk, eg_last=eg_last,
                          o_ref=o_refs[si], head0=si * C_HEADS + p * HEADS_PER_PACK,
                          xm=-nmat, tm=jnp.where(eye_w, 1.0, 0.0) - nmat))

    for pk in packs:
        pk["bd"] = _block_diag(pk["xm"], bd_mask_w)
    for _ in range(5):
        for pk in packs:
            pk["xm"] = jnp.dot(pk["xm"].astype(BF16), pk["bd"], preferred_element_type=F32)
        for pk in packs:
            pk["bd"] = _block_diag(pk["xm"], bd_mask_w)
        for pk in packs:
            pk["tm"] = pk["tm"] + jnp.dot(pk["tm"].astype(BF16), pk["bd"], preferred_element_type=F32)

    for pk in packs:
        tmb = pk["tm"].astype(BF16)
        pk["u"] = jnp.dot(tmb, _block_diag(pk["vb"], bd_mask_c), preferred_element_type=F32)
        pk["w"] = jnp.dot(tmb, _block_diag(pk["kbg"], bd_mask_c), preferred_element_type=F32)

    for pk in packs:
        v_new, q_s = [], []
        for j in range(HEADS_PER_PACK):
            hs = slice(j * C_DK, (j + 1) * C_DK)
            ws_qs = _bdot(jnp.concatenate([pk["w"][:, hs], pk["qg"][:, hs]], axis=0), s_ref[pk["head0"] + j])
            v_new.append(pk["u"][:, hs] - ws_qs[:cc])
            q_s.append(ws_qs[cc:])
        pk["v_new"] = v_new
        pk["q_s"] = jnp.concatenate(q_s, axis=1)

    for pk in packs:
        v_new_c = jnp.concatenate(pk["v_new"], axis=1)
        o = pk["q_s"] + jnp.dot(pk["a_qk"].astype(BF16), _block_diag(v_new_c, bd_mask_c), preferred_element_type=F32)
        pk["o_ref"][:, pk["sl"]] = o.astype(BF16)
        for j in range(HEADS_PER_PACK):
            hs = slice(j * C_DK, (j + 1) * C_DK)
            h = pk["head0"] + j
            c = pk["cols"][j]
            s_ref[h] = s_ref[h] * pk["eg_last"][:, c:c + 1] + _bdot_tn(pk["kdec"][:, hs], pk["v_new"][j])


def _delta_scan(qkv, gates, a_log, dt_bias, *, nbatch, seq, n_ctx):
    tt = qkv.shape[0]
    cc = C_CHUNK
    lc, nc = n_ctx // cc, seq // cc
    n_lat_blk = nbatch * nc

    def row_blk(reverse, b, s):
        if reverse:
            return jnp.where(s < lc, n_lat_blk + b * lc + (lc - 1 - s), b * nc + (nc - 1 - (s - lc)))
        return jnp.where(s < lc, n_lat_blk + b * lc + s, b * nc + (s - lc))

    def spec(reverse, width, col):
        return pl.BlockSpec((cc, width), lambda b, s: (row_blk(reverse, b, s), col))

    pad = jnp.zeros((1, GATE_W - 2 * C_HEADS), F32)
    alog_row = jnp.concatenate([a_log.reshape(1, 2 * C_HEADS), pad], axis=1)
    dtb_row = jnp.concatenate([dt_bias.reshape(1, 2 * C_HEADS), pad], axis=1)
    in_specs = []
    for reverse in (False, True):
        in_specs += [spec(reverse, C_W, 0), spec(reverse, C_W, 1), spec(reverse, C_W, 2), spec(reverse, GATE_W, 0)]
    in_specs += [pl.BlockSpec((1, GATE_W), lambda b, s: (0, 0))] * 2
    return pl.pallas_call(
        _delta_kernel,
        grid=(nbatch, lc + nc),
        in_specs=in_specs,
        out_specs=[spec(False, C_W, 0), spec(True, C_W, 0)],
        out_shape=[jax.ShapeDtypeStruct((tt, C_W), BF16)] * 2,
        scratch_shapes=[pltpu.VMEM((2 * C_HEADS, C_DK, C_DK), F32)],
        compiler_params=_params("arbitrary", "arbitrary"),
        name="delta_scan",
    )(qkv, qkv, qkv, gates, qkv, qkv, qkv, gates, alog_row, dtb_row)


def _gated_norm_kernel(of_ref, ob_ref, z_ref, nw_ref, o_ref):
    nw = nw_ref[...]
    for h in range(C_HEADS):
        sl = slice(h * C_DK, (h + 1) * C_DK)
        o = of_ref[:, sl].astype(F32) + ob_ref[:, sl].astype(F32)
        on = o * lax.rsqrt(jnp.mean(o * o, axis=-1, keepdims=True) + EPS) * nw
        o_ref[:, sl] = (on * _silu(z_ref[:, sl].astype(F32))).astype(o_ref.dtype)


def _gated_norm(o_f, o_b, p, norm_w):
    tt = o_f.shape[0]
    tm = _tile(512, tt)
    zcol = 3 * C_W // C_W
    return pl.pallas_call(
        _gated_norm_kernel,
        grid=(tt // tm,),
        in_specs=[pl.BlockSpec((tm, C_W), lambda i: (i, 0)),
                  pl.BlockSpec((tm, C_W), lambda i: (i, 0)),
                  pl.BlockSpec((tm, C_W), lambda i: (i, zcol)),
                  pl.BlockSpec((1, C_DK), lambda i: (0, 0))],
        out_specs=pl.BlockSpec((tm, C_W), lambda i: (i, 0)),
        out_shape=jax.ShapeDtypeStruct((tt, C_W), BF16),
        compiler_params=_params("arbitrary"),
        name="gated_norm",
    )(o_f, o_b, p, norm_w.reshape(1, C_DK))


def _max_first(vals):
    bv = vals[0]
    bi = jnp.zeros(vals[0].shape, jnp.int32)
    for idx in range(1, len(vals)):
        upd = vals[idx] > bv
        bi = jnp.where(upd, idx, bi)
        bv = jnp.where(upd, vals[idx], bv)
    return bv, bi


def _pick(vals, idx):
    out = vals[0]
    for n in range(1, len(vals)):
        out = jnp.where(idx == n, vals[n], out)
    return out


def _router_kernel(x_ref, mod_ref, nw_ref, rwt_ref, rb_ref, v_ref, route_ref):
    v = _norm_mod(x_ref[...], nw_ref[...], mod_ref[...], 3, 4)
    v_ref[...] = v
    vh, vl = _split(v)
    wh, wl = _split(rwt_ref[...])
    logits = _bdot_nt(wh, vh) + _bdot_nt(wh, vl) + _bdot_nt(wl, vh)
    scores = _sigmoid(logits)
    sel = scores + rb_ref[...]
    sel_r = [sel[e:e + 1, :] for e in range(N_EXPERTS)]
    sc_r = [scores[e:e + 1, :] for e in range(N_EXPERTS)]
    epg = EXPERTS_PER_GROUP
    gscore = []
    for g in range(N_GROUPS):
        a, b, c, d = sel_r[g * epg:(g + 1) * epg]
        m_ab, n_ab = jnp.maximum(a, b), jnp.minimum(a, b)
        m_cd, n_cd = jnp.maximum(c, d), jnp.minimum(c, d)
        top1 = jnp.maximum(m_ab, m_cd)
        top2 = jnp.maximum(jnp.minimum(m_ab, m_cd), jnp.maximum(n_ab, n_cd))
        gscore.append(top1 + top2)
    _, best = _max_first(gscore)
    in_sel = [_pick([sel_r[g * epg + k] for g in range(N_GROUPS)], best) for k in range(epg)]
    in_sc = [_pick([sc_r[g * epg + k] for g in range(N_GROUPS)], best) for k in range(epg)]
    _, i1 = _max_first(in_sel)
    _, i2 = _max_first([jnp.where(i1 == k, -jnp.inf, in_sel[k]) for k in range(epg)])
    s1 = _pick(in_sc, i1)
    s2 = _pick(in_sc, i2)
    tot = s1 + s2
    swap = i1 > i2
    e_lo = best * epg + jnp.where(swap, i2, i1)
    e_hi = best * epg + jnp.where(swap, i1, i2)
    w_lo = jnp.where(swap, s2, s1) / tot
    w_hi = jnp.where(swap, s1, s2) / tot
    rows = [e_lo.astype(F32), e_hi.astype(F32), w_lo, w_hi] + [jnp.zeros_like(s1)] * 4
    route_ref[...] = jnp.concatenate(rows, axis=0)


def _router(xs, mod3, nw, router_w, router_b, *, n_lat, seq):
    tt, d = xs.shape
    nb = mod3.shape[0] - 1
    tm = _row_tile(256, seq, tt, n_lat)
    per_seq = seq // tm
    nt = tt // tm
    return pl.pallas_call(
        _router_kernel,
        grid=(nt,),
        in_specs=[pl.BlockSpec((tm, d), lambda i: (i, 0)),
                  pl.BlockSpec((None, 6, d), lambda i: (jnp.minimum(i // per_seq, nb), 0, 0)),
                  pl.BlockSpec((1, d), lambda i: (0, 0)),
                  pl.BlockSpec((N_EXPERTS, d), lambda i: (0, 0)),
                  pl.BlockSpec((N_EXPERTS, 1), lambda i: (0, 0))],
        out_specs=[pl.BlockSpec((tm, d), lambda i: (i, 0)),
                   pl.BlockSpec((None, 8, tm), lambda i: (i, 0, 0))],
        out_shape=[jax.ShapeDtypeStruct((tt, d), F32),
                   jax.ShapeDtypeStruct((nt, 8, tm), F32)],
        compiler_params=_params("arbitrary"),
        name="router",
    )(xs, mod3, nw, router_w.T, router_b.reshape(N_EXPERTS, 1))


def _load_tile_indices(dest_hbm, idx_smem, sem, n):
    cp = pltpu.make_async_copy(dest_hbm.at[pl.ds(pl.program_id(0) * n, n)], idx_smem, sem)
    cp.start()
    cp.wait()


def _dispatch_kernel(dest_hbm, v_ref, init_hbm, xs_hbm, idx_smem, sem_idx, sem, *, tm):
    del init_hbm
    _load_tile_indices(dest_hbm, idx_smem, sem_idx, idx_smem.shape[0])

    def row_copy(t, d):
        return pltpu.make_async_copy(v_ref.at[pl.ds(t, 1)], xs_hbm.at[pl.ds(d, 1)], sem)

    def issue(t, c):
        row_copy(t, idx_smem[t]).start()
        return c

    lax.fori_loop(0, tm, issue, 0, unroll=8)
    pltpu.make_async_copy(v_ref, xs_hbm.at[pl.ds(0, tm)], sem).wait()


def _dispatch(dest_tiles, v, xs_init, *, tm):
    tt, d = v.shape
    n_idx = dest_tiles.shape[0] // (tt // tm)
    return pl.pallas_call(
        functools.partial(_dispatch_kernel, tm=tm),
        grid=(tt // tm,),
        in_specs=[pl.BlockSpec(memory_space=pl.ANY),
                  pl.BlockSpec((tm, d), lambda i: (i, 0)),
                  pl.BlockSpec(memory_space=pl.ANY)],
        out_specs=pl.BlockSpec(memory_space=pl.ANY),
        out_shape=jax.ShapeDtypeStruct(xs_init.shape, F32),
        scratch_shapes=[pltpu.SMEM((n_idx,), jnp.int32), pltpu.SemaphoreType.DMA(()), pltpu.SemaphoreType.DMA(())],
        input_output_aliases={2: 0},
        compiler_params=_params("arbitrary"),
        name="moe_dispatch",
    )(dest_tiles, v, xs_init)


def _expert_kernel(wt_ref, we_ref, ws_ref, wv_ref, x_ref, wg_ref, wu_ref, wd_ref, o_ref):
    del wt_ref, we_ref, ws_ref
    n = pl.program_id(0)

    @pl.when(wv_ref[n] > 0)
    def _():
        x = x_ref[...].astype(BF16)
        hg = jnp.dot(x, wg_ref[...], preferred_element_type=F32)
        hu = jnp.dot(x, wu_ref[...], preferred_element_type=F32)
        act = (_silu(hg) * hu).astype(BF16)
        o_ref[...] = jnp.dot(act, wd_ref[...], preferred_element_type=F32)

    @pl.when(wv_ref[n] == 0)
    def _():
        o_ref[...] = jnp.zeros(o_ref.shape, F32)


def _experts(work_tile, work_expert, work_slot, work_valid, xs, wg, wu, wd, *, tm, layer):
    rows, d = xs.shape
    dff = wg.shape[3]
    grid_spec = pltpu.PrefetchScalarGridSpec(
        num_scalar_prefetch=4,
        grid=(work_tile.shape[0],),
        in_specs=[pl.BlockSpec((tm, d), lambda n, wt, we, ws, wv: (wt[n], 0)),
                  pl.BlockSpec((None, None, d, dff), lambda n, wt, we, ws, wv: (layer, we[n], 0, 0)),
                  pl.BlockSpec((None, None, d, dff), lambda n, wt, we, ws, wv: (layer, we[n], 0, 0)),
                  pl.BlockSpec((None, None, dff, d), lambda n, wt, we, ws, wv: (layer, we[n], 0, 0))],
        out_specs=pl.BlockSpec((tm, d), lambda n, wt, we, ws, wv: (wt[n], ws[n])),
    )
    return pl.pallas_call(
        _expert_kernel,
        grid_spec=grid_spec,
        out_shape=jax.ShapeDtypeStruct((rows, 2 * d), F32),
        compiler_params=_params("arbitrary"),
        name="moe_experts",
    )(work_tile, work_expert, work_slot, work_valid, xs, wg, wu, wd)


def _combine_kernel(dest_hbm, ys_hbm, x_ref, w_ref, mod_ref, *rest, tm):
    fnw_ref = rest[0] if len(rest) == 6 else None
    o_ref, idx_smem, y_ref, sem_idx, sem = rest[-5:]
    _load_tile_indices(dest_hbm, idx_smem, sem_idx, idx_smem.shape[0])
    d = x_ref.shape[1]

    def issue(t, c):
        pltpu.make_async_copy(ys_hbm.at[pl.ds(idx_smem[t], 1)], y_ref.at[pl.ds(t, 1)], sem).start()
        return c

    lax.fori_loop(0, tm, issue, 0, unroll=8)
    pltpu.make_async_copy(ys_hbm.at[pl.ds(0, tm)], y_ref, sem).wait()
    w = w_ref[...]
    ff = w[:, 0:1] * y_ref[:, :d] + w[:, 1:2] * y_ref[:, d:]
    out = x_ref[...] + mod_ref[5:6, :] * ff
    if fnw_ref is not None:
        out = out * lax.rsqrt(jnp.mean(out * out, axis=-1, keepdims=True) + EPS) * fnw_ref[...]
    o_ref[...] = out


def _combine(dest_tiles, ys, xs, w12, mod3, final_nw, *, tm, n_lat, seq):
    tt, d = xs.shape
    nb = mod3.shape[0] - 1
    per_seq = seq // tm
    n_idx = dest_tiles.shape[0] // (tt // tm)
    in_specs = [pl.BlockSpec(memory_space=pl.ANY),
                pl.BlockSpec(memory_space=pl.ANY),
                pl.BlockSpec((tm, d), lambda i: (i, 0)),
                pl.BlockSpec((tm, 2), lambda i: (i, 0)),
                pl.BlockSpec((None, 6, d), lambda i: (jnp.minimum(i // per_seq, nb), 0, 0))]
    args = [dest_tiles, ys, xs, w12, mod3]
    if final_nw is not None:
        in_specs.append(pl.BlockSpec((1, d), lambda i: (0, 0)))
        args.append(final_nw.reshape(1, d))
    return pl.pallas_call(
        functools.partial(_combine_kernel, tm=tm),
        grid=(tt // tm,),
        in_specs=in_specs,
        out_specs=pl.BlockSpec((tm, d), lambda i: (i, 0)),
        out_shape=jax.ShapeDtypeStruct((tt, d), F32),
        scratch_shapes=[pltpu.SMEM((n_idx,), jnp.int32), pltpu.VMEM((tm, 2 * d), F32),
                        pltpu.SemaphoreType.DMA(()), pltpu.SemaphoreType.DMA(())],
        compiler_params=_params("arbitrary"),
        name="moe_combine",
    )(*args)


PAIRS_PER_GROUP = EXPERTS_PER_GROUP * (EXPERTS_PER_GROUP - 1) // 2
N_CLASSES = N_GROUPS * PAIRS_PER_GROUP
IDX_ALIGN = 1024


def _moe_sorted_rows(tt):
    tme = _tile(256, tt)
    return tme, tt // tme + N_CLASSES


def _moe(xs, xs_sorted_init, mod3, nw, router_w, router_b, wg, wu, wd, final_nw, *, n_lat, seq, layer):
    tt, d = xs.shape
    v, route = _router(xs, mod3, nw, router_w, router_b, n_lat=n_lat, seq=seq)
    route = route.transpose(1, 0, 2).reshape(8, tt)
    e_lo, e_hi = route[0].astype(jnp.int32), route[1].astype(jnp.int32)
    w12 = route[2:4].T

    tme, n_tiles = _moe_sorted_rows(tt)
    epg = EXPERTS_PER_GROUP
    lo, hi = e_lo % epg, e_hi % epg
    cls = (e_lo // epg) * PAIRS_PER_GROUP + (lo * (2 * epg - 1 - lo)) // 2 + (hi - lo - 1)
    onehot = (cls[:, None] == jnp.arange(N_CLASSES, dtype=jnp.int32)[None, :]).astype(jnp.int32)
    csum = jnp.cumsum(onehot, axis=0)
    rank = jnp.sum(onehot * csum, axis=1) - 1
    cls_tiles = (csum[-1] + tme - 1) // tme
    tile_end = jnp.cumsum(cls_tiles)
    tile_start = tile_end - cls_tiles
    dest = jnp.sum(onehot * (tile_start * tme)[None, :], axis=1) + rank

    pairs = [(a, b) for a in range(epg) for b in range(a + 1, epg)]
    segs = sorted((g * epg + pair[side], g * PAIRS_PER_GROUP + pi, side)
                  for g in range(N_GROUPS) for pi, pair in enumerate(pairs) for side in (0, 1))
    seg_expert = jnp.array([s[0] for s in segs], jnp.int32)
    seg_cls = jnp.array([s[1] for s in segs], jnp.int32)
    seg_side = jnp.array([s[2] for s in segs], jnp.int32)
    seg_len = cls_tiles[seg_cls]
    seg_end = jnp.cumsum(seg_len)
    seg_start = seg_end - seg_len
    w_idx = jnp.arange(2 * n_tiles, dtype=jnp.int32)
    w_seg = jnp.minimum(jnp.sum((w_idx[:, None] >= seg_end[None, :]).astype(jnp.int32), axis=1), len(segs) - 1)
    work_valid = (w_idx < seg_end[-1]).astype(jnp.int32)
    work_tile = jnp.where(work_valid > 0, tile_start[seg_cls[w_seg]] + w_idx - seg_start[w_seg], n_tiles - 1)
    work_slot = seg_side[w_seg] * work_valid
    work_expert = seg_expert[w_seg]

    tmd = _row_tile(512, seq, tt, n_lat)
    n_idx = IDX_ALIGN * ((tmd + IDX_ALIGN - 1) // IDX_ALIGN)
    dest_tiles = jnp.pad(dest.reshape(tt // tmd, tmd), ((0, 0), (0, n_idx - tmd))).reshape(-1)
    xs_sorted = _dispatch(dest_tiles, v, xs_sorted_init, tm=tmd)
    ys = _experts(work_tile, work_expert, work_slot, work_valid, xs_sorted, wg, wu, wd, tm=tme, layer=layer)
    return _combine(dest_tiles, ys, xs, w12, mod3, final_nw, tm=tmd, n_lat=n_lat, seq=seq), xs_sorted


def _rope_tables(seq):
    rows = seq // GRID_W
    r = jnp.repeat(jnp.arange(rows, dtype=F32), GRID_W)
    col = jnp.tile(jnp.arange(GRID_W, dtype=F32), rows)
    n_freq = HEAD_DIM // 4
    inv = ROPE_BASE ** (-jnp.arange(n_freq, dtype=F32) / n_freq)
    ar, ac = r[:, None] * inv, col[:, None] * inv
    cos_t = jnp.concatenate([jnp.cos(ar), jnp.cos(ar), jnp.cos(ac), jnp.cos(ac)], axis=1)
    sin_t = jnp.concatenate([-jnp.sin(ar), jnp.sin(ar), -jnp.sin(ac), jnp.sin(ac)], axis=1)
    return cos_t, sin_t


def _even_layer(xs, mod3, nw, w_in, w_out, sink, rope, fconst_lat, fconst_ctx, *, nbatch, seq, n_ctx, n_out_rows):
    n_lat = nbatch * seq
    cos_t, sin_t = rope
    p = _norm_mm(xs, mod3, nw, w_in.astype(BF16), cos_t, sin_t, n_lat=n_lat, seq=seq, rope_cols=A_Q_W + A_KV_W)
    a = _win_attn(p, sink, nbatch=nbatch, seq=seq, n_ctx=n_ctx)
    a = _ctx_attn(p, sink, a, nbatch=nbatch, n_lat=n_lat, n_ctx=n_ctx)
    f = _fourier(p, *fconst_lat, None, nbatch=nbatch, t=seq, row_blk0=0)
    f = _fourier(p, *fconst_ctx, f, nbatch=nbatch, t=n_ctx, row_blk0=n_lat // n_ctx)
    wo = w_out.astype(BF16)
    return _res_mm([a, f], [wo[:A_Q_W], wo[A_Q_W:]], xs, mod3, gate_row=2, n_lat=n_lat, seq=seq,
                   n_out_rows=n_out_rows)


def _odd_layer(xs, mod3, nw, w_in, conv_w, a_log, dt_bias, norm_w, w_out, rope, *, nbatch, seq, n_ctx, n_out_rows):
    n_lat = nbatch * seq
    d = xs.shape[1]
    cos_t, sin_t = rope
    n_main = 4 * C_W
    w_main = w_in[:, :n_main].astype(BF16)
    w_gate = jnp.concatenate([w_in[:, n_main:], jnp.zeros((d, GATE_W - 4 * C_HEADS), F32)], axis=1).astype(BF16)
    p, gates = _norm_mm(xs, mod3, nw, w_main, cos_t, sin_t, n_lat=n_lat, seq=seq, rope_cols=0, w_side=w_gate)
    qkv = _conv_prep(p, conv_w, None, nseq=nbatch, t=seq, row_blk0=0)
    qkv = _conv_prep(p, conv_w, qkv, nseq=nbatch, t=n_ctx, row_blk0=n_lat // n_ctx)
    o_f, o_b = _delta_scan(qkv, gates, a_log, dt_bias, nbatch=nbatch, seq=seq, n_ctx=n_ctx)
    on = _gated_norm(o_f, o_b, p, norm_w)
    return _res_mm([on], [w_out.astype(BF16)], xs, mod3, gate_row=2, n_lat=n_lat, seq=seq, n_out_rows=n_out_rows)


def kernel(x, c, ctx, c_ctx, adaln_w, adaln_b, norm_mix_w, norm_ffn_w, attn_in_w, attn_out_w, attn_sink, dn_in_w, dn_conv_w, dn_a_log, dn_dt_bias, dn_norm_w, dn_out_w, router_w, router_b, exp_gate_w, exp_up_w, exp_down_w, final_norm_w):
    nbatch, seq, d = x.shape
    n_ctx = ctx.shape[1]
    depth = adaln_w.shape[0]
    n_lat = nbatch * seq
    assert seq % BLOCK == 0 and seq % GRID_W == 0 and n_ctx % C_CHUNK == 0 and n_lat % n_ctx == 0

    xs = jnp.concatenate([x.reshape(n_lat, d), ctx.reshape(nbatch * n_ctx, d)], axis=0)
    mod_rows = 8 * ((nbatch + 1 + 7) // 8)
    c_all = jnp.concatenate([c, c_ctx[None, :], jnp.zeros((mod_rows - nbatch - 1, d), F32)], axis=0)
    mod_all = _adaln(c_all, adaln_w, adaln_b)[:, :nbatch + 1].reshape(depth, nbatch + 1, 6, d)

    rope = _rope_tables(seq)
    fconst_lat = _fourier_consts(seq)
    fconst_ctx = _fourier_consts(n_ctx)
    tme, n_tiles = _moe_sorted_rows(xs.shape[0])
    xs_sorted = jnp.zeros((n_tiles * tme, d), F32)
    wg_all, wu_all, wd_all = exp_gate_w.astype(BF16), exp_up_w.astype(BF16), exp_down_w.astype(BF16)

    for layer in range(depth):
        mod3 = mod_all[layer]
        i = layer // 2
        nw = norm_mix_w[layer].reshape(1, d)
        last = layer == depth - 1
        n_out_rows = n_lat if last else xs.shape[0]
        if layer % 2 == 0:
            xs = _even_layer(xs, mod3, nw, attn_in_w[i], attn_out_w[i], attn_sink[i], rope, fconst_lat, fconst_ctx,
                             nbatch=nbatch, seq=seq, n_ctx=n_ctx, n_out_rows=n_out_rows)
        else:
            xs = _odd_layer(xs, mod3, nw, dn_in_w[i], dn_conv_w[i], dn_a_log[i], dn_dt_bias[i], dn_norm_w[i],
                            dn_out_w[i], rope, nbatch=nbatch, seq=seq, n_ctx=n_ctx, n_out_rows=n_out_rows)
        xs, xs_sorted = _moe(xs, xs_sorted, mod3, norm_ffn_w[layer].reshape(1, d), router_w, router_b,
                             wg_all, wu_all, wd_all, final_norm_w if last else None,
                             n_lat=n_lat, seq=seq, layer=layer)
    return xs.reshape(nbatch, seq, d)
```
